```python
import math
import jax, jax.numpy as jnp
from jax import lax
import numpy as np

D_MODEL = 1024
BATCH = 8
SEQ = 4096
DEPTH = 1

N_ATTN_HEADS = 8
ATTN_HEAD_DIM = 64
ATTN_V_DIM = 2 * ATTN_HEAD_DIM
ATTN_WIDTH = N_ATTN_HEADS * ATTN_V_DIM
QK_COLS = N_ATTN_HEADS * 2 * ATTN_HEAD_DIM
Q_BLOCK = 128
ROPE_THETA = 10000.0
SSD_HEADS = 16
SSD_HEAD_DIM = 64
SSD_WIDTH = SSD_HEADS * SSD_HEAD_DIM
SSD_GROUPS = 4
SSD_HEADS_PER_GROUP = SSD_HEADS // SSD_GROUPS
SSD_STATE = 128
SSD_CONV = 4
SSD_CONV_CH = SSD_WIDTH + 2 * SSD_GROUPS * SSD_STATE
SSD_CHUNK = 128
MIX_WIDTH = ATTN_WIDTH + SSD_WIDTH
IN_COLS = 2 * QK_COLS + ATTN_WIDTH + SSD_WIDTH + SSD_CONV_CH + SSD_HEADS
D_FF = 2816
NORM_EPS = 1e-6

kernel_name = "hymba_diffattn_ssd_macaron"


def rms_norm(x, w):
    xf = x.astype(jnp.float32)
    y = xf * lax.rsqrt(jnp.mean(xf * xf, axis=-1, keepdims=True) + NORM_EPS)
    return (y * w.astype(jnp.float32)).astype(x.dtype)


def swiglu(h, w_gate, w_up, w_down):
    return (jax.nn.silu(h @ w_gate) * (h @ w_up)) @ w_down


def rope_tables(positions, dim):
    inv = 1.0 / (ROPE_THETA ** (jnp.arange(0, dim, 2, dtype=jnp.float32) / dim))
    ang = positions.astype(jnp.float32)[..., None] * inv
    ang = jnp.concatenate([ang, ang], axis=-1)
    return jnp.cos(ang), jnp.sin(ang)


def apply_rope(t, cos, sin):
    c = cos[:, :, None, None, :]
    s = sin[:, :, None, None, :]
    t1, t2 = jnp.split(t, 2, axis=-1)
    rot = jnp.concatenate([-t2, t1], axis=-1)
    return (t.astype(jnp.float32) * c + rot.astype(jnp.float32) * s).astype(t.dtype)


def diff_attention(q, k, v, lam):
    s = q.shape[1]
    qh = jnp.transpose(q, (0, 2, 3, 1, 4)) * (ATTN_HEAD_DIM ** -0.5)
    kh = jnp.transpose(k, (0, 2, 3, 1, 4))
    vh = jnp.transpose(v, (0, 2, 1, 3))
    outs = []
    for start in range(0, s, Q_BLOCK):
        end = start + Q_BLOCK
        sc = jnp.einsum('bhcqd,bhckd->bhcqk', qh[:, :, :, start:end], kh[:, :, :, :end]).astype(jnp.float32)
        causal = (start + jnp.arange(Q_BLOCK))[:, None] >= jnp.arange(end)[None, :]
        p = jax.nn.softmax(jnp.where(causal, sc, -jnp.inf), axis=-1)
        w = p[:, :, 0] - lam * p[:, :, 1]
        outs.append(jnp.einsum('bhqk,bhkv->bhqv', w.astype(v.dtype), vh[:, :, :end]))
    o = jnp.concatenate(outs, axis=2)
    return jnp.transpose(o, (0, 2, 1, 3))


def causal_depthwise_conv(u, w, b):
    out = lax.conv_general_dilated(u, w[:, None, :].astype(u.dtype), window_strides=(1,),
                                   padding=[(SSD_CONV - 1, 0)],
                                   dimension_numbers=('NWC', 'WIO', 'NWC'),
                                   feature_group_count=u.shape[-1])
    return out + b.astype(u.dtype)


def ssd_chunked(xs, dt, a, bmat, cmat):
    b, s, g, r, p = xs.shape
    n = bmat.shape[-1]
    l = SSD_CHUNK
    c = s // l
    f32 = jnp.float32
    X = (xs.astype(f32) * dt[..., None]).reshape(b, c, l, g, r, p)
    dA = (dt * a).reshape(b, c, l, g, r)
    Bc = bmat.astype(f32).reshape(b, c, l, g, n)
    Cc = cmat.astype(f32).reshape(b, c, l, g, n)
    a_cum = jnp.cumsum(dA, axis=2)
    ac = jnp.moveaxis(a_cum, 2, -1)
    seg = ac[..., :, None] - ac[..., None, :]
    tril = jnp.arange(l)[:, None] >= jnp.arange(l)[None, :]
    decay = jnp.exp(jnp.where(tril, seg, -jnp.inf))
    cb = jnp.einsum('bclgn,bcsgn->bcgls', Cc, Bc)
    y_diag = jnp.einsum('bcgrls,bcsgrp->bclgrp', cb[:, :, :, None] * decay, X)
    decay_to_end = jnp.exp(a_cum[:, :, -1:] - a_cum)
    chunk_states = jnp.einsum('bclgn,bclgrp->bcgrpn', Bc, X * decay_to_end[..., None])
    chunk_decay = jnp.exp(a_cum[:, :, -1])

    def step(h, inp):
        st, dec = inp
        return h * dec[..., None, None] + st, h

    h0 = jnp.zeros((b, g, r, p, n), f32)
    _, states_in = lax.scan(step, h0, (jnp.moveaxis(chunk_states, 1, 0), jnp.moveaxis(chunk_decay, 1, 0)))
    states_in = jnp.moveaxis(states_in, 0, 1)
    y_off = jnp.einsum('bclgn,bcgrpn->bclgrp', Cc, states_in) * jnp.exp(a_cum)[..., None]
    return (y_diag + y_off).reshape(b, s, g, r, p)


def hybrid_mixer(u, cos, sin, w_in, lambda_q1, lambda_k1, lambda_q2, lambda_k2, attn_subln, lambda_init,
                 conv_w, conv_b, dt_bias, a_log, d_skip, ssd_norm, w_out):
    b, s, _ = u.shape
    proj = u @ w_in
    splits = np.cumsum([QK_COLS, QK_COLS, ATTN_WIDTH, SSD_WIDTH, SSD_CONV_CH]).tolist()
    q, k, v, z, xbc, dt_raw = jnp.split(proj, splits, axis=-1)
    q = apply_rope(q.reshape(b, s, N_ATTN_HEADS, 2, ATTN_HEAD_DIM), cos, sin)
    k = apply_rope(k.reshape(b, s, N_ATTN_HEADS, 2, ATTN_HEAD_DIM), cos, sin)
    v = v.reshape(b, s, N_ATTN_HEADS, ATTN_V_DIM)
    f32 = jnp.float32
    lam = (jnp.exp(jnp.sum(lambda_q1.astype(f32) * lambda_k1.astype(f32)))
           - jnp.exp(jnp.sum(lambda_q2.astype(f32) * lambda_k2.astype(f32))) + lambda_init)
    o = diff_attention(q, k, v, lam)
    o = (rms_norm(o, attn_subln) * (1.0 - lambda_init)).reshape(b, s, ATTN_WIDTH)
    xbc = jax.nn.silu(causal_depthwise_conv(xbc, conv_w, conv_b))
    xs, bm, cm = jnp.split(xbc, [SSD_WIDTH, SSD_WIDTH + SSD_GROUPS * SSD_STATE], axis=-1)
    xs = xs.reshape(b, s, SSD_GROUPS, SSD_HEADS_PER_GROUP, SSD_HEAD_DIM)
    bm = bm.reshape(b, s, SSD_GROUPS, SSD_STATE)
    cm = cm.reshape(b, s, SSD_GROUPS, SSD_STATE)
    dt = jax.nn.softplus(dt_raw.astype(f32) + dt_bias.astype(f32)).reshape(b, s, SSD_GROUPS, SSD_HEADS_PER_GROUP)
    a = -jnp.exp(a_log.astype(f32)).reshape(SSD_GROUPS, SSD_HEADS_PER_GROUP)
    y = ssd_chunked(xs, dt, a, bm, cm)
    y = y + d_skip.astype(f32).reshape(SSD_GROUPS, SSD_HEADS_PER_GROUP)[..., None] * xs.astype(f32)
    y = y.reshape(b, s, SSD_WIDTH).astype(u.dtype) * jax.nn.silu(z)
    y = rms_norm(y.reshape(b, s, SSD_GROUPS, SSD_WIDTH // SSD_GROUPS),
                 ssd_norm.reshape(SSD_GROUPS, SSD_WIDTH // SSD_GROUPS)).reshape(b, s, SSD_WIDTH)
    return jnp.concatenate([o, y], axis=-1) @ w_out


def setup_inputs(seed: int = 0) -> dict:
    key = jax.random.key(seed)
    ks = jax.random.split(key, 24)
    f32 = jnp.float32
    L = DEPTH

    def nrm(k, shape, scale):
        return jax.random.normal(k, shape, f32) * scale

    def gain(k, shape):
        return 1.0 + 0.02 * jax.random.normal(k, shape, f32)

    x = jax.random.normal(ks[0], (BATCH, SEQ, D_MODEL), f32)
    offset = jax.random.randint(ks[1], (BATCH, 1), 0, 1024, dtype=jnp.int32)
    positions = offset + jnp.arange(SEQ, dtype=jnp.int32)[None, :]
    dt_init = jnp.exp(jax.random.uniform(ks[2], (L, SSD_HEADS), f32) * (math.log(0.1) - math.log(0.001)) + math.log(0.001))
    dt_init = jnp.maximum(dt_init, 1e-4)
    dt_bias = dt_init + jnp.log(-jnp.expm1(-dt_init))
    a_log = jnp.log(jax.random.uniform(ks[3], (L, SSD_HEADS), f32, 1.0, 16.0))
    return {
        "x": x,
        "positions": positions,
        "ffn1_norm": gain(ks[4], (L, D_MODEL)),
        "ffn1_w_gate": nrm(ks[5], (L, D_MODEL, D_FF), D_MODEL ** -0.5),
        "ffn1_w_up": nrm(ks[6], (L, D_MODEL, D_FF), D_MODEL ** -0.5),
        "ffn1_w_down": nrm(ks[7], (L, D_FF, D_MODEL), D_FF ** -0.5),
        "mix_norm": gain(ks[8], (L, D_MODEL)),
        "w_in": nrm(ks[9], (L, D_MODEL, IN_COLS), D_MODEL ** -0.5),
        "lambda_q1": nrm(ks[10], (L, ATTN_HEAD_DIM), 0.1),
        "lambda_k1": nrm(ks[11], (L, ATTN_HEAD_DIM), 0.1),
        "lambda_q2": nrm(ks[12], (L, ATTN_HEAD_DIM), 0.1),
        "lambda_k2": nrm(ks[13], (L, ATTN_HEAD_DIM), 0.1),
        "attn_subln": gain(ks[14], (L, ATTN_V_DIM)),
        "conv_w": jax.random.uniform(ks[15], (L, SSD_CONV, SSD_CONV_CH), f32, -0.5, 0.5),
        "conv_b": nrm(ks[16], (L, SSD_CONV_CH), 0.02),
        "dt_bias": dt_bias,
        "a_log": a_log,
        "d_skip": gain(ks[17], (L, SSD_HEADS)),
        "ssd_norm": gain(ks[18], (L, SSD_WIDTH)),
        "w_out": nrm(ks[19], (L, MIX_WIDTH, D_MODEL), MIX_WIDTH ** -0.5),
        "ffn2_norm": gain(ks[20], (L, D_MODEL)),
        "ffn2_w_gate": nrm(ks[21], (L, D_MODEL, D_FF), D_MODEL ** -0.5),
        "ffn2_w_up": nrm(ks[22], (L, D_MODEL, D_FF), D_MODEL ** -0.5),
        "ffn2_w_down": nrm(ks[23], (L, D_FF, D_MODEL), D_FF ** -0.5),
        "final_norm": gain(jax.random.fold_in(key, 99), (D_MODEL,)),
    }


def reference(x, positions, ffn1_norm, ffn1_w_gate, ffn1_w_up, ffn1_w_down, mix_norm, w_in,
              lambda_q1, lambda_k1, lambda_q2, lambda_k2, attn_subln, conv_w, conv_b, dt_bias, a_log,
              d_skip, ssd_norm, w_out, ffn2_norm, ffn2_w_gate, ffn2_w_up, ffn2_w_down, final_norm):
    cos, sin = rope_tables(positions, ATTN_HEAD_DIM)
    h = x
    for layer in range(DEPTH):
        lambda_init = 0.8 - 0.6 * math.exp(-0.3 * layer)
        h = h + 0.5 * swiglu(rms_norm(h, ffn1_norm[layer]), ffn1_w_gate[layer], ffn1_w_up[layer], ffn1_w_down[layer])
        h = h + hybrid_mixer(rms_norm(h, mix_norm[layer]), cos, sin, w_in[layer],
                             lambda_q1[layer], lambda_k1[layer], lambda_q2[layer], lambda_k2[layer],
                             attn_subln[layer], lambda_init, conv_w[layer], conv_b[layer], dt_bias[layer],
                             a_log[layer], d_skip[layer], ssd_norm[layer], w_out[layer])
        h = h + 0.5 * swiglu(rms_norm(h, ffn2_norm[layer]), ffn2_w_gate[layer], ffn2_w_up[layer], ffn2_w_down[layer])
    return rms_norm(h, final_norm)
```

```python
import functools
import math

import jax
import jax.numpy as jnp
from jax import lax
from jax.experimental import pallas as pl
from jax.experimental.pallas import tpu as pltpu

NORM_EPS = 1e-6
ROPE_THETA = 10000.0
N_ATTN_HEADS = 8
ATTN_HEAD_DIM = 64
ATTN_V_DIM = 2 * ATTN_HEAD_DIM
SSD_HEADS = 16
SSD_HEAD_DIM = 64
SSD_GROUPS = 4
SSD_HEADS_PER_GROUP = SSD_HEADS // SSD_GROUPS
SSD_STATE = 128
SSD_CONV = 4
SSD_CHUNK = 128
SSD_GROUP_WIDTH = SSD_HEADS_PER_GROUP * SSD_HEAD_DIM

LANES = 128
SUBLANES = 8
VMEM_LIMIT_BYTES = 56 * 1024 * 1024

TOKEN_TILE = 512
ATTN_BLOCK = 256
MASK_VALUE = -1e30

_NT = (((1,), (1,)), ((), ()))

bf16 = jnp.bfloat16
f32 = jnp.float32


def _rms(x, w):
    return x * lax.rsqrt(jnp.mean(x * x, axis=-1, keepdims=True) + NORM_EPS) * w


def _silu(x):
    return x * jax.nn.sigmoid(x)


def _dot(a, b):
    return jnp.dot(a, b, preferred_element_type=f32)


def _const_spec(shape):
    zeros = (0,) * len(shape)
    return pl.BlockSpec(shape, lambda *_: zeros, pipeline_mode=pl.Buffered(1))


def _swiglu(xn, wg_ref, wu_ref, wd_ref):
    g = _dot(xn, wg_ref[...])
    u = _dot(xn, wu_ref[...])
    a = (_silu(g) * u).astype(bf16)
    return _dot(a, wd_ref[...])


def _ffn1_kernel(x_ref, nw_ref, wg_ref, wu_ref, wd_ref, o_ref):
    x = x_ref[...]
    xn = _rms(x, nw_ref[...]).astype(bf16)
    o_ref[...] = x + 0.5 * _swiglu(xn, wg_ref, wu_ref, wd_ref)


def _ffn1(x, nw, wg, wu, wd):
    t, d = x.shape
    tm = TOKEN_TILE
    return pl.pallas_call(
        _ffn1_kernel,
        grid=(t // tm,),
        in_specs=[pl.BlockSpec((tm, d), lambda i: (i, 0)),
                  _const_spec(nw.shape), _const_spec(wg.shape), _const_spec(wu.shape),
                  _const_spec(wd.shape)],
        out_specs=pl.BlockSpec((tm, d), lambda i: (i, 0)),
        out_shape=jax.ShapeDtypeStruct((t, d), f32),
        compiler_params=pltpu.CompilerParams(
            dimension_semantics=("arbitrary",), vmem_limit_bytes=VMEM_LIMIT_BYTES),
        name="ffn1",
    )(x, nw, wg, wu, wd)


def _in_proj_kernel(h_ref, pos_ref, invf_ref, nw_ref, wq_ref, wk_ref, wv_ref, wz_ref, wx_ref,
                    wdt_ref, q_ref, k_ref, v_ref, z_ref, x_ref, dt_ref):
    u = _rms(h_ref[...], nw_ref[...]).astype(bf16)
    ang = pos_ref[...].astype(f32) * invf_ref[...]
    cos = jnp.cos(ang)
    sin = jnp.sin(ang)
    lane = lax.broadcasted_iota(jnp.int32, ang.shape, 1)
    first_half = (lane % ATTN_HEAD_DIM) < (ATTN_HEAD_DIM // 2)
    sin_signed = jnp.where(first_half, -sin, sin)

    def rope(t):
        rot = jnp.where(first_half,
                        pltpu.roll(t, LANES - ATTN_HEAD_DIM // 2, axis=1),
                        pltpu.roll(t, ATTN_HEAD_DIM // 2, axis=1))
        return t * cos + rot * sin_signed

    q = _dot(u, wq_ref[...])
    k = _dot(u, wk_ref[...])
    q_scale = ATTN_HEAD_DIM ** -0.5
    for h in range(N_ATTN_HEADS):
        sl = slice(h * LANES, (h + 1) * LANES)
        q_ref[:, sl] = (rope(q[:, sl]) * q_scale).astype(bf16)
        k_ref[:, sl] = rope(k[:, sl]).astype(bf16)
    v_ref[...] = _dot(u, wv_ref[...]).astype(bf16)
    z_ref[...] = _dot(u, wz_ref[...]).astype(bf16)
    x_ref[...] = _dot(u, wx_ref[...]).astype(bf16)
    dt_ref[...] = _dot(u, wdt_ref[...])


def _in_proj(h, pos, invf, nw, wq, wk, wv, wz, wx, wdt):
    t, d = h.shape
    tm = TOKEN_TILE
    row = lambda i: (i, 0)
    outs = [(wq.shape[1], bf16), (wk.shape[1], bf16), (wv.shape[1], bf16), (wz.shape[1], bf16),
            (wx.shape[1], bf16), (wdt.shape[1], f32)]
    return pl.pallas_call(
        _in_proj_kernel,
        grid=(t // tm,),
        in_specs=[pl.BlockSpec((tm, d), row), pl.BlockSpec((tm, 1), row),
                  _const_spec(invf.shape), _const_spec(nw.shape), _const_spec(wq.shape),
                  _const_spec(wk.shape), _const_spec(wv.shape), _const_spec(wz.shape),
                  _const_spec(wx.shape), _const_spec(wdt.shape)],
        out_specs=[pl.BlockSpec((tm, n), row) for n, _ in outs],
        out_shape=[jax.ShapeDtypeStruct((t, n), dt) for n, dt in outs],
        compiler_params=pltpu.CompilerParams(
            dimension_semantics=("arbitrary",), vmem_limit_bytes=VMEM_LIMIT_BYTES),
        name="in_proj",
    )(h, pos, invf, nw, wq, wk, wv, wz, wx, wdt)


def _lam_kernel(lambda_init, q1_ref, k1_ref, q2_ref, k2_ref, o_ref):
    s1 = jnp.sum(q1_ref[...] * k1_ref[...], axis=-1, keepdims=True)
    s2 = jnp.sum(q2_ref[...] * k2_ref[...], axis=-1, keepdims=True)
    o_ref[...] = jnp.broadcast_to(jnp.exp(s1) - jnp.exp(s2) + lambda_init, o_ref.shape)


def _lam(lq1, lk1, lq2, lk2, lambda_init):
    return pl.pallas_call(
        functools.partial(_lam_kernel, lambda_init),
        out_shape=jax.ShapeDtypeStruct((1, LANES), f32),
        name="lam",
    )(lq1, lk1, lq2, lk2)


def _attn_kernel(out_scale, lam_ref, q_ref, k_ref, v_ref, sw_ref, o_ref,
                 m1_ref, l1_ref, a1_ref, m2_ref, l2_ref, a2_ref):
    qi = pl.program_id(2)
    blk = ATTN_BLOCK
    q = q_ref[...]
    lane = lax.broadcasted_iota(jnp.int32, q.shape, 1)
    zero = jnp.zeros_like(q)
    q1 = jnp.where(lane < ATTN_HEAD_DIM, q, zero)
    q2 = jnp.where(lane < ATTN_HEAD_DIM, zero, q)

    for m_ref, l_ref, a_ref in ((m1_ref, l1_ref, a1_ref), (m2_ref, l2_ref, a2_ref)):
        m_ref[...] = jnp.full(m_ref.shape, MASK_VALUE, f32)
        l_ref[...] = jnp.zeros(l_ref.shape, f32)
        a_ref[...] = jnp.zeros(a_ref.shape, f32)

    def update(qc, kb, vb, mask, m_ref, l_ref, a_ref):
        s = lax.dot_general(qc, kb, _NT, preferred_element_type=f32)
        if mask is not None:
            s = jnp.where(mask, s, MASK_VALUE)
        m_old = m_ref[...]
        m_new = jnp.maximum(m_old, jnp.max(s, axis=-1, keepdims=True))
        alpha = jnp.exp(m_old - m_new)
        p = jnp.exp(s - m_new)
        l_ref[...] = alpha * l_ref[...] + jnp.sum(p, axis=-1, keepdims=True)
        a_ref[...] = alpha * a_ref[...] + _dot(p.astype(bf16), vb)
        m_ref[...] = m_new

    def step(j, mask):
        start = pl.multiple_of(j * blk, blk)
        kb = k_ref[pl.ds(start, blk), :]
        vb = v_ref[pl.ds(start, blk), :]
        update(q1, kb, vb, mask, m1_ref, l1_ref, a1_ref)
        update(q2, kb, vb, mask, m2_ref, l2_ref, a2_ref)

    def body(j, carry):
        step(j, None)
        return carry

    lax.fori_loop(0, qi, body, 0)
    row = lax.broadcasted_iota(jnp.int32, (blk, blk), 0)
    col = lax.broadcasted_iota(jnp.int32, (blk, blk), 1)
    step(qi, row >= col)

    o = a1_ref[...] / l1_ref[...] - lam_ref[...] * (a2_ref[...] / l2_ref[...])
    o_ref[...] = (_rms(o, sw_ref[...]) * out_scale).astype(bf16)


def _attention(lam, q, k, v, subln, batch, seq, out_scale):
    t = q.shape[0]
    blk = ATTN_BLOCK
    nq = seq // blk
    stat = pltpu.VMEM((blk, 1), f32)
    acc = pltpu.VMEM((blk, ATTN_V_DIM), f32)
    return pl.pallas_call(
        functools.partial(_attn_kernel, out_scale),
        grid=(batch, N_ATTN_HEADS, nq),
        in_specs=[pl.BlockSpec((1, LANES), lambda b, h, i: (0, 0)),
                  pl.BlockSpec((blk, LANES), lambda b, h, i: (b * nq + i, h)),
                  pl.BlockSpec((seq, LANES), lambda b, h, i: (b, h)),
                  pl.BlockSpec((seq, ATTN_V_DIM), lambda b, h, i: (b, h)),
                  pl.BlockSpec((1, ATTN_V_DIM), lambda b, h, i: (0, 0))],
        out_specs=pl.BlockSpec((blk, ATTN_V_DIM), lambda b, h, i: (b * nq + i, h)),
        out_shape=jax.ShapeDtypeStruct((t, N_ATTN_HEADS * ATTN_V_DIM), bf16),
        scratch_shapes=[stat, stat, acc, stat, stat, acc],
        compiler_params=pltpu.CompilerParams(
            dimension_semantics=("arbitrary", "arbitrary", "arbitrary")),
        name="diff_attn",
    )(lam, q, k, v, subln)


def _split_bf16(x, parts):
    out = []
    for _ in range(parts - 1):
        hi = x.astype(bf16)
        out.append(hi)
        x = x - hi.astype(f32)
    out.append(x.astype(bf16))
    return out


def _ssd_kernel(xbc_ref, dt_ref, z_ref, cw_ref, cb_ref, dtb_ref, alog_ref, dskip_ref, nw_ref,
                o_ref, xext_ref, state_ref):
    c = pl.program_id(1)
    L = SSD_CHUNK
    width = SSD_HEADS * SSD_HEAD_DIM
    bc_width = SSD_GROUPS * SSD_STATE

    @pl.when(c == 0)
    def _():
        xext_ref[0:SUBLANES, :] = jnp.zeros((SUBLANES, xext_ref.shape[1]), f32)
        state_ref[...] = jnp.zeros(state_ref.shape, f32)

    @pl.when(c > 0)
    def _():
        xext_ref[0:SUBLANES, :] = xext_ref[L:L + SUBLANES, :]

    xext_ref[SUBLANES:SUBLANES + L, :] = xbc_ref[...].astype(f32)

    conv = cb_ref[...] + cw_ref[0:1, :] * xext_ref[pl.ds(SUBLANES - (SSD_CONV - 1), L), :]
    for kk in range(1, SSD_CONV):
        conv = conv + cw_ref[kk:kk + 1, :] * xext_ref[pl.ds(SUBLANES - (SSD_CONV - 1) + kk, L), :]
    act = _silu(conv)
    xs = act[:, :width]
    bm = act[:, width:width + bc_width]
    cm = act[:, width + bc_width:]

    dt = jax.nn.softplus(dt_ref[...] + dtb_ref[...])
    d_a = dt * (-jnp.exp(alog_ref[...]))
    row = lax.broadcasted_iota(jnp.int32, (L, L), 0)
    col = lax.broadcasted_iota(jnp.int32, (L, L), 1)
    tril = row >= col
    tril_b = tril.astype(bf16)
    a_cum = sum(_dot(tril_b, part) for part in _split_bf16(d_a, 3))
    a_cum_t = a_cum.T
    a_end = a_cum[L - 1:L, :]
    decay_to_end = jnp.exp(a_end - a_cum)
    exp_a = jnp.exp(a_cum)

    hrow = lax.broadcasted_iota(jnp.int32, (LANES, width), 0)
    hcol = lax.broadcasted_iota(jnp.int32, (LANES, width), 1)
    expand = (hcol // SSD_HEAD_DIM == hrow).astype(bf16)
    dt_x = _dot(dt.astype(bf16), expand)
    dte_x = _dot(decay_to_end.astype(bf16), expand)
    ea_hi, ea_lo = _split_bf16(exp_a, 2)
    ea_x = _dot(ea_hi, expand) + _dot(ea_lo, expand)

    x_dt = xs * dt_x
    x_dt_b = x_dt.astype(bf16)
    x_end_b = (x_dt * dte_x).astype(bf16)
    gcol = lax.broadcasted_iota(jnp.int32, (L, SSD_GROUP_WIDTH), 1) // SSD_HEAD_DIM
    zero_g = jnp.zeros((L, SSD_GROUP_WIDTH), bf16)

    y = dskip_ref[...] * xs
    y_parts = []
    for g in range(SSD_GROUPS):
        gs = slice(g * SSD_GROUP_WIDTH, (g + 1) * SSD_GROUP_WIDTH)
        ns = slice(g * SSD_STATE, (g + 1) * SSD_STATE)
        b_g = bm[:, ns]
        c_g = cm[:, ns].astype(bf16)
        cb = lax.dot_general(c_g, b_g.astype(bf16), _NT, preferred_element_type=f32)
        b_t = b_g.T.astype(bf16)
        st = state_ref[g]
        y_g = _dot(c_g, st.astype(bf16)) * ea_x[:, gs]
        state_ref[g] = st * ea_x[L - 1:L, gs] + _dot(b_t, x_end_b[:, gs])
        x_g = x_dt_b[:, gs]
        for r in range(SSD_HEADS_PER_GROUP):
            h = g * SSD_HEADS_PER_GROUP + r
            seg = a_cum[:, h:h + 1] - a_cum_t[h:h + 1, :]
            decay = jnp.exp(jnp.where(tril, seg, MASK_VALUE))
            m = (cb * decay).astype(bf16)
            y_g = y_g + _dot(m, jnp.where(gcol == r, x_g, zero_g))
        y_parts.append(y_g)

    for g in range(SSD_GROUPS):
        gs = slice(g * SSD_GROUP_WIDTH, (g + 1) * SSD_GROUP_WIDTH)
        y_g = (y[:, gs] + y_parts[g]) * _silu(z_ref[:, gs].astype(f32))
        o_ref[:, gs] = _rms(y_g, nw_ref[:, gs]).astype(bf16)


def _ssd(xbc, dt_raw, z, conv_w, conv_b, dt_bias, a_log, d_skip, ssd_norm, batch, seq):
    t, ch = xbc.shape
    L = SSD_CHUNK
    nc = seq // L
    width = SSD_HEADS * SSD_HEAD_DIM
    row = lambda b, c: (b * nc + c, 0)
    return pl.pallas_call(
        _ssd_kernel,
        grid=(batch, nc),
        in_specs=[pl.BlockSpec((L, ch), row), pl.BlockSpec((L, LANES), row),
                  pl.BlockSpec((L, width), row),
                  _const_spec(conv_w.shape), _const_spec(conv_b.shape), _const_spec(dt_bias.shape),
                  _const_spec(a_log.shape), _const_spec(d_skip.shape), _const_spec(ssd_norm.shape)],
        out_specs=pl.BlockSpec((L, width), row),
        out_shape=jax.ShapeDtypeStruct((t, width), bf16),
        scratch_shapes=[pltpu.VMEM((L + SUBLANES, ch), f32),
                        pltpu.VMEM((SSD_GROUPS, SSD_STATE, SSD_GROUP_WIDTH), f32)],
        compiler_params=pltpu.CompilerParams(dimension_semantics=("arbitrary", "arbitrary")),
        name="ssd",
    )(xbc, dt_raw, z, conv_w, conv_b, dt_bias, a_log, d_skip, ssd_norm)


def _out_ffn2_kernel(final, h_ref, o_ref, y_ref, woo_ref, woy_ref, nw_ref, wg_ref, wu_ref, wd_ref,
                     fw_ref, out_ref):
    h2 = h_ref[...] + _dot(o_ref[...], woo_ref[...]) + _dot(y_ref[...], woy_ref[...])
    xn = _rms(h2, nw_ref[...]).astype(bf16)
    h3 = h2 + 0.5 * _swiglu(xn, wg_ref, wu_ref, wd_ref)
    out_ref[...] = _rms(h3, fw_ref[...]) if final else h3


def _out_ffn2(h, o, y, woo, woy, nw, wg, wu, wd, fw, final):
    t, d = h.shape
    tm = TOKEN_TILE
    row = lambda i: (i, 0)
    return pl.pallas_call(
        functools.partial(_out_ffn2_kernel, final),
        grid=(t // tm,),
        in_specs=[pl.BlockSpec((tm, d), row), pl.BlockSpec((tm, o.shape[1]), row),
                  pl.BlockSpec((tm, y.shape[1]), row),
                  _const_spec(woo.shape), _const_spec(woy.shape), _const_spec(nw.shape),
                  _const_spec(wg.shape), _const_spec(wu.shape), _const_spec(wd.shape),
                  _const_spec(fw.shape)],
        out_specs=pl.BlockSpec((tm, d), row),
        out_shape=jax.ShapeDtypeStruct((t, d), f32),
        compiler_params=pltpu.CompilerParams(
            dimension_semantics=("arbitrary",), vmem_limit_bytes=VMEM_LIMIT_BYTES),
        name="out_ffn2",
    )(h, o, y, woo, woy, nw, wg, wu, wd, fw)


def kernel(x, positions, ffn1_norm, ffn1_w_gate, ffn1_w_up, ffn1_w_down, mix_norm, w_in, lambda_q1, lambda_k1, lambda_q2, lambda_k2, attn_subln, conv_w, conv_b, dt_bias, a_log, d_skip, ssd_norm, w_out, ffn2_norm, ffn2_w_gate, ffn2_w_up, ffn2_w_down, final_norm):
    batch, seq, d = x.shape
    depth = ffn1_norm.shape[0]
    t = batch * seq
    qk_cols = N_ATTN_HEADS * 2 * ATTN_HEAD_DIM
    attn_width = N_ATTN_HEADS * ATTN_V_DIM
    ssd_width = SSD_HEADS * SSD_HEAD_DIM
    conv_ch = ssd_width + 2 * SSD_GROUPS * SSD_STATE
    splits = [0, qk_cols, 2 * qk_cols, 2 * qk_cols + attn_width, 2 * qk_cols + attn_width + ssd_width,
              2 * qk_cols + attn_width + ssd_width + conv_ch]

    inv = 1.0 / (ROPE_THETA ** (jnp.arange(0, ATTN_HEAD_DIM, 2, dtype=f32) / ATTN_HEAD_DIM))
    invf = jnp.tile(inv, LANES // inv.shape[0])[None, :]
    pos = positions.reshape(t, 1)
    pad_heads = lambda a: jnp.pad(a.astype(f32), (0, LANES - SSD_HEADS))[None, :]
    row2d = lambda a: a.astype(f32)[None, :]

    h = x.reshape(t, d)
    for layer in range(depth):
        lambda_init = 0.8 - 0.6 * math.exp(-0.3 * layer)
        last = layer == depth - 1
        h1 = _ffn1(h, row2d(ffn1_norm[layer]), ffn1_w_gate[layer].astype(bf16),
                   ffn1_w_up[layer].astype(bf16), ffn1_w_down[layer].astype(bf16))

        w = w_in[layer]
        wq, wk, wv, wz, wx = (w[:, splits[i]:splits[i + 1]].astype(bf16) for i in range(5))
        wdt = jnp.pad(w[:, splits[5]:], ((0, 0), (0, LANES - SSD_HEADS))).astype(bf16)
        q, k, v, z, xbc, dt_raw = _in_proj(h1, pos, invf, row2d(mix_norm[layer]),
                                           wq, wk, wv, wz, wx, wdt)

        lam = _lam(row2d(lambda_q1[layer]), row2d(lambda_k1[layer]), row2d(lambda_q2[layer]),
                   row2d(lambda_k2[layer]), lambda_init)
        o = _attention(lam, q, k, v, row2d(attn_subln[layer]), batch, seq, 1.0 - lambda_init)

        y = _ssd(xbc, dt_raw, z, conv_w[layer].astype(f32), row2d(conv_b[layer]),
                 pad_heads(dt_bias[layer]), pad_heads(a_log[layer]),
                 row2d(jnp.repeat(d_skip[layer], SSD_HEAD_DIM)), row2d(ssd_norm[layer]), batch, seq)

        wo = w_out[layer].astype(bf16)
        h = _out_ffn2(h1, o, y, wo[:attn_width], wo[attn_width:], row2d(ffn2_norm[layer]),
                      ffn2_w_gate[layer].astype(bf16), ffn2_w_up[layer].astype(bf16),
                      ffn2_w_down[layer].astype(bf16), row2d(final_norm), final=last)
    return h.reshape(batch, seq, d)
```

```python
import functools
import math

import jax
import jax.numpy as jnp
from jax import lax
from jax.experimental import pallas as pl
from jax.experimental.pallas import tpu as pltpu

NORM_EPS = 1e-6
ROPE_THETA = 10000.0
N_ATTN_HEADS = 8
ATTN_HEAD_DIM = 64
ATTN_V_DIM = 2 * ATTN_HEAD_DIM
SSD_HEADS = 16
SSD_HEAD_DIM = 64
SSD_GROUPS = 4
SSD_HEADS_PER_GROUP = SSD_HEADS // SSD_GROUPS
SSD_STATE = 128
SSD_CONV = 4
SSD_CHUNK = 128
SSD_GROUP_WIDTH = SSD_HEADS_PER_GROUP * SSD_HEAD_DIM

LANES = 128
SUBLANES = 8
VMEM_LIMIT_BYTES = 56 * 1024 * 1024

TOKEN_TILE = 512
ATTN_BLOCK = 512
MASK_VALUE = -1e30

_NT = (((1,), (1,)), ((), ()))

bf16 = jnp.bfloat16
f32 = jnp.float32


def _rms(x, w):
    return x * lax.rsqrt(jnp.mean(x * x, axis=-1, keepdims=True) + NORM_EPS) * w


def _silu(x):
    return x * jax.nn.sigmoid(x)


def _dot(a, b):
    return jnp.dot(a, b, preferred_element_type=f32)


def _const_spec(shape):
    zeros = (0,) * len(shape)
    return pl.BlockSpec(shape, lambda *_: zeros, pipeline_mode=pl.Buffered(1))


def _swiglu(xn, wg_ref, wu_ref, wd_ref):
    g = _dot(xn, wg_ref[...])
    u = _dot(xn, wu_ref[...])
    a = (_silu(g) * u).astype(bf16)
    return _dot(a, wd_ref[...])


def _ffn1_kernel(x_ref, nw_ref, wg_ref, wu_ref, wd_ref, o_ref):
    x = x_ref[...]
    xn = _rms(x, nw_ref[...]).astype(bf16)
    o_ref[...] = x + 0.5 * _swiglu(xn, wg_ref, wu_ref, wd_ref)


def _ffn1(x, nw, wg, wu, wd):
    t, d = x.shape
    tm = TOKEN_TILE
    return pl.pallas_call(
        _ffn1_kernel,
        grid=(t // tm,),
        in_specs=[pl.BlockSpec((tm, d), lambda i: (i, 0)),
                  _const_spec(nw.shape), _const_spec(wg.shape), _const_spec(wu.shape),
                  _const_spec(wd.shape)],
        out_specs=pl.BlockSpec((tm, d), lambda i: (i, 0)),
        out_shape=jax.ShapeDtypeStruct((t, d), f32),
        compiler_params=pltpu.CompilerParams(
            dimension_semantics=("arbitrary",), vmem_limit_bytes=VMEM_LIMIT_BYTES),
        name="ffn1",
    )(x, nw, wg, wu, wd)


def _in_proj_kernel(h_ref, pos_ref, invf_ref, nw_ref, wq_ref, wk_ref, wvt_ref, wz_ref, wx_ref,
                    wdt_ref, q_ref, k_ref, vt_ref, z_ref, x_ref, dt_ref):
    u = _rms(h_ref[...], nw_ref[...]).astype(bf16)
    ang = pos_ref[...].astype(f32) * invf_ref[...]
    cos = jnp.cos(ang)
    sin = jnp.sin(ang)
    lane = lax.broadcasted_iota(jnp.int32, ang.shape, 1)
    first_half = (lane % ATTN_HEAD_DIM) < (ATTN_HEAD_DIM // 2)
    sin_signed = jnp.where(first_half, -sin, sin)

    def rope(t):
        rot = jnp.where(first_half,
                        pltpu.roll(t, LANES - ATTN_HEAD_DIM // 2, axis=1),
                        pltpu.roll(t, ATTN_HEAD_DIM // 2, axis=1))
        return t * cos + rot * sin_signed

    q = _dot(u, wq_ref[...])
    k = _dot(u, wk_ref[...])
    q_scale = ATTN_HEAD_DIM ** -0.5 * math.log2(math.e)
    for h in range(N_ATTN_HEADS):
        sl = slice(h * LANES, (h + 1) * LANES)
        q_ref[:, sl] = (rope(q[:, sl]) * q_scale).astype(bf16)
        k_ref[:, sl] = rope(k[:, sl]).astype(bf16)
    for j in range(vt_ref.shape[0]):
        ub = u[j * ATTN_BLOCK:(j + 1) * ATTN_BLOCK, :]
        vt_ref[j] = lax.dot_general(wvt_ref[...], ub, _NT, preferred_element_type=f32).astype(bf16)
    z_ref[...] = _dot(u, wz_ref[...]).astype(bf16)
    x_ref[...] = _dot(u, wx_ref[...]).astype(bf16)
    dt_ref[...] = _dot(u, wdt_ref[...])


def _in_proj(h, pos, invf, nw, wq, wk, wvt, wz, wx, wdt):
    t, d = h.shape
    tm = TOKEN_TILE
    kv_per_tile = tm // ATTN_BLOCK
    row = lambda i: (i, 0)

    def row_out(n, dtype):
        return pl.BlockSpec((tm, n), row), jax.ShapeDtypeStruct((t, n), dtype)

    outs = [row_out(wq.shape[1], bf16), row_out(wk.shape[1], bf16),
            (pl.BlockSpec((kv_per_tile, wvt.shape[0], ATTN_BLOCK), lambda i: (i, 0, 0)),
             jax.ShapeDtypeStruct((t // ATTN_BLOCK, wvt.shape[0], ATTN_BLOCK), bf16)),
            row_out(wz.shape[1], bf16), row_out(wx.shape[1], bf16), row_out(wdt.shape[1], f32)]
    return pl.pallas_call(
        _in_proj_kernel,
        grid=(t // tm,),
        in_specs=[pl.BlockSpec((tm, d), row), pl.BlockSpec((tm, 1), row),
                  _const_spec(invf.shape), _const_spec(nw.shape), _const_spec(wq.shape),
                  _const_spec(wk.shape), _const_spec(wvt.shape), _const_spec(wz.shape),
                  _const_spec(wx.shape), _const_spec(wdt.shape)],
        out_specs=[spec for spec, _ in outs],
        out_shape=[shape for _, shape in outs],
        compiler_params=pltpu.CompilerParams(
            dimension_semantics=("arbitrary",), vmem_limit_bytes=VMEM_LIMIT_BYTES),
        name="in_proj",
    )(h, pos, invf, nw, wq, wk, wvt, wz, wx, wdt)


def _lam_kernel(lambda_init, q1_ref, k1_ref, q2_ref, k2_ref, o_ref):
    s1 = jnp.sum(q1_ref[...] * k1_ref[...], axis=-1, keepdims=True)
    s2 = jnp.sum(q2_ref[...] * k2_ref[...], axis=-1, keepdims=True)
    o_ref[...] = jnp.broadcast_to(jnp.exp(s1) - jnp.exp(s2) + lambda_init, o_ref.shape)


def _lam(lq1, lk1, lq2, lk2, lambda_init):
    return pl.pallas_call(
        functools.partial(_lam_kernel, lambda_init),
        out_shape=jax.ShapeDtypeStruct((1, LANES), f32),
        name="lam",
    )(lq1, lk1, lq2, lk2)


def _attn_kernel(out_scale, lam_ref, q_ref, k_ref, vt_ref, sw_ref, o_ref,
                 qm_ref, sa_ref, sb_ref, m_ref, l_ref, acc_ref):
    qi = pl.program_id(2)
    blk = ATTN_BLOCK
    half = blk // 2
    q = q_ref[...]
    lane = lax.broadcasted_iota(jnp.int32, q.shape, 1)
    zero = jnp.zeros_like(q)
    qm_ref[0] = jnp.where(lane < ATTN_HEAD_DIM, q, zero)
    qm_ref[1] = jnp.where(lane < ATTN_HEAD_DIM, zero, q)

    m_ref[...] = jnp.full(m_ref.shape, MASK_VALUE, f32)
    l_ref[...] = jnp.zeros(l_ref.shape, f32)
    acc_ref[...] = jnp.zeros(acc_ref.shape, f32)

    def scores(j, dst_ref):
        kb = k_ref[pl.ds(pl.multiple_of(j * blk, blk), blk), :]
        for c in range(2):
            dst_ref[c] = lax.dot_general(kb, qm_ref[c], _NT, preferred_element_type=f32)

    def update(c, hq, s, vtb):
        qs = slice(hq * half, (hq + 1) * half)
        m_old = m_ref[c, :, qs]
        m_new = jnp.maximum(m_old, jnp.max(s, axis=0, keepdims=True))
        alpha = jnp.exp2(m_old - m_new)
        p = jnp.exp2(s - m_new)
        l_ref[c, :, qs] = alpha * l_ref[c, :, qs] + jnp.sum(p, axis=0, keepdims=True)
        acc_ref[c, :, qs] = alpha * acc_ref[c, :, qs] + _dot(vtb, p.astype(bf16))
        m_ref[c, :, qs] = m_new

    def full_block(j, src_ref):
        vtb = vt_ref[j]
        for c in range(2):
            for hq in range(2):
                update(c, hq, src_ref[c, :, hq * half:(hq + 1) * half], vtb)

    def diagonal_block(j, src_ref):
        vtb = vt_ref[j]
        key = lax.broadcasted_iota(jnp.int32, (half, half), 0)
        qry = lax.broadcasted_iota(jnp.int32, (half, half), 1)
        tri = key <= qry
        for c in range(2):
            update(c, 0, jnp.where(tri, src_ref[c, 0:half, 0:half], MASK_VALUE), vtb[:, 0:half])
            s_lo = jnp.where(tri, src_ref[c, half:blk, half:blk], MASK_VALUE)
            update(c, 1, jnp.concatenate([src_ref[c, 0:half, half:blk], s_lo], axis=0), vtb)

    scores(0, sa_ref)

    def pair(jj, carry):
        j = 2 * jj
        scores(j + 1, sb_ref)
        full_block(j, sa_ref)
        scores(j + 2, sa_ref)
        full_block(j + 1, sb_ref)
        return carry

    lax.fori_loop(0, qi // 2, pair, 0)

    @pl.when(qi % 2 == 0)
    def _():
        diagonal_block(qi, sa_ref)

    @pl.when(qi % 2 == 1)
    def _():
        scores(qi, sb_ref)
        full_block(qi - 1, sa_ref)
        diagonal_block(qi, sb_ref)

    lam = lam_ref[:, 0:1]
    o_t = acc_ref[0] * (1.0 / l_ref[0]) - lam * (acc_ref[1] * (1.0 / l_ref[1]))
    o_ref[...] = (_rms(o_t.T, sw_ref[...]) * out_scale).astype(bf16)


def _attention(lam, q, k, vt, subln, batch, seq, out_scale):
    t = q.shape[0]
    blk = ATTN_BLOCK
    nq = seq // blk
    return pl.pallas_call(
        functools.partial(_attn_kernel, out_scale),
        grid=(batch, N_ATTN_HEADS, nq),
        in_specs=[pl.BlockSpec((1, LANES), lambda b, h, i: (0, 0)),
                  pl.BlockSpec((blk, LANES), lambda b, h, i: (b * nq + i, h)),
                  pl.BlockSpec((seq, LANES), lambda b, h, i: (b, h)),
                  pl.BlockSpec((nq, ATTN_V_DIM, blk), lambda b, h, i: (b, h, 0)),
                  pl.BlockSpec((1, ATTN_V_DIM), lambda b, h, i: (0, 0))],
        out_specs=pl.BlockSpec((blk, ATTN_V_DIM), lambda b, h, i: (b * nq + i, h)),
        out_shape=jax.ShapeDtypeStruct((t, N_ATTN_HEADS * ATTN_V_DIM), bf16),
        scratch_shapes=[pltpu.VMEM((2, blk, LANES), bf16),
                        pltpu.VMEM((2, blk, blk), f32), pltpu.VMEM((2, blk, blk), f32),
                        pltpu.VMEM((2, 1, blk), f32), pltpu.VMEM((2, 1, blk), f32),
                        pltpu.VMEM((2, ATTN_V_DIM, blk), f32)],
        compiler_params=pltpu.CompilerParams(
            dimension_semantics=("arbitrary", "arbitrary", "arbitrary")),
        name="diff_attn",
    )(lam, q, k, vt, subln)


def _split_bf16(x, parts):
    out = []
    for _ in range(parts - 1):
        hi = x.astype(bf16)
        out.append(hi)
        x = x - hi.astype(f32)
    out.append(x.astype(bf16))
    return out


def _ssd_kernel(xbc_ref, dt_ref, z_ref, cw_ref, cb_ref, dtb_ref, alog_ref, dskip_ref, nw_ref,
                o_ref, xext_ref, state_ref):
    c = pl.program_id(1)
    L = SSD_CHUNK
    width = SSD_HEADS * SSD_HEAD_DIM
    bc_width = SSD_GROUPS * SSD_STATE

    @pl.when(c == 0)
    def _():
        xext_ref[0:SUBLANES, :] = jnp.zeros((SUBLANES, xext_ref.shape[1]), f32)
        state_ref[...] = jnp.zeros(state_ref.shape, f32)

    @pl.when(c > 0)
    def _():
        xext_ref[0:SUBLANES, :] = xext_ref[L:L + SUBLANES, :]

    xext_ref[SUBLANES:SUBLANES + L, :] = xbc_ref[...].astype(f32)

    conv = cb_ref[...] + cw_ref[0:1, :] * xext_ref[pl.ds(SUBLANES - (SSD_CONV - 1), L), :]
    for kk in range(1, SSD_CONV):
        conv = conv + cw_ref[kk:kk + 1, :] * xext_ref[pl.ds(SUBLANES - (SSD_CONV - 1) + kk, L), :]
    act = _silu(conv)
    xs = act[:, :width]
    bm = act[:, width:width + bc_width]
    cm = act[:, width + bc_width:]

    dt = jax.nn.softplus(dt_ref[...] + dtb_ref[...])
    d_a = dt * (-jnp.exp(alog_ref[...]))
    row = lax.broadcasted_iota(jnp.int32, (L, L), 0)
    col = lax.broadcasted_iota(jnp.int32, (L, L), 1)
    tril = row >= col
    tril_b = tril.astype(bf16)
    a_cum = sum(_dot(tril_b, part) for part in _split_bf16(d_a, 3))
    a_cum_t = a_cum.T
    a_end = a_cum[L - 1:L, :]
    decay_to_end = jnp.exp(a_end - a_cum)
    exp_a = jnp.exp(a_cum)

    hrow = lax.broadcasted_iota(jnp.int32, (LANES, width), 0)
    hcol = lax.broadcasted_iota(jnp.int32, (LANES, width), 1)
    expand = (hcol // SSD_HEAD_DIM == hrow).astype(bf16)
    dt_x = _dot(dt.astype(bf16), expand)
    dte_x = _dot(decay_to_end.astype(bf16), expand)
    ea_hi, ea_lo = _split_bf16(exp_a, 2)
    ea_x = _dot(ea_hi, expand) + _dot(ea_lo, expand)

    x_dt = xs * dt_x
    x_dt_b = x_dt.astype(bf16)
    x_end_b = (x_dt * dte_x).astype(bf16)
    gcol = lax.broadcasted_iota(jnp.int32, (L, SSD_GROUP_WIDTH), 1) // SSD_HEAD_DIM
    zero_g = jnp.zeros((L, SSD_GROUP_WIDTH), bf16)

    y = dskip_ref[...] * xs
    y_parts = []
    for g in range(SSD_GROUPS):
        gs = slice(g * SSD_GROUP_WIDTH, (g + 1) * SSD_GROUP_WIDTH)
        ns = slice(g * SSD_STATE, (g + 1) * SSD_STATE)
        b_g = bm[:, ns]
        c_g = cm[:, ns].astype(bf16)
        cb = lax.dot_general(c_g, b_g.astype(bf16), _NT, preferred_element_type=f32)
        b_t = b_g.T.astype(bf16)
        st = state_ref[g]
        y_g = _dot(c_g, st.astype(bf16)) * ea_x[:, gs]
        state_ref[g] = st * ea_x[L - 1:L, gs] + _dot(b_t, x_end_b[:, gs])
        x_g = x_dt_b[:, gs]
        for r in range(SSD_HEADS_PER_GROUP):
            h = g * SSD_HEADS_PER_GROUP + r
            seg = a_cum[:, h:h + 1] - a_cum_t[h:h + 1, :]
            decay = jnp.exp(jnp.where(tril, seg, MASK_VALUE))
            m = (cb * decay).astype(bf16)
            y_g = y_g + _dot(m, jnp.where(gcol == r, x_g, zero_g))
        y_parts.append(y_g)

    for g in range(SSD_GROUPS):
        gs = slice(g * SSD_GROUP_WIDTH, (g + 1) * SSD_GROUP_WIDTH)
        y_g = (y[:, gs] + y_parts[g]) * _silu(z_ref[:, gs].astype(f32))
        o_ref[:, gs] = _rms(y_g, nw_ref[:, gs]).astype(bf16)


def _ssd(xbc, dt_raw, z, conv_w, conv_b, dt_bias, a_log, d_skip, ssd_norm, batch, seq):
    t, ch = xbc.shape
    L = SSD_CHUNK
    nc = seq // L
    width = SSD_HEADS * SSD_HEAD_DIM
    row = lambda b, c: (b * nc + c, 0)
    return pl.pallas_call(
        _ssd_kernel,
        grid=(batch, nc),
        in_specs=[pl.BlockSpec((L, ch), row), pl.BlockSpec((L, LANES), row),
                  pl.BlockSpec((L, width), row),
                  _const_spec(conv_w.shape), _const_spec(conv_b.shape), _const_spec(dt_bias.shape),
                  _const_spec(a_log.shape), _const_spec(d_skip.shape), _const_spec(ssd_norm.shape)],
        out_specs=pl.BlockSpec((L, width), row),
        out_shape=jax.ShapeDtypeStruct((t, width), bf16),
        scratch_shapes=[pltpu.VMEM((L + SUBLANES, ch), f32),
                        pltpu.VMEM((SSD_GROUPS, SSD_STATE, SSD_GROUP_WIDTH), f32)],
        compiler_params=pltpu.CompilerParams(dimension_semantics=("arbitrary", "arbitrary")),
        name="ssd",
    )(xbc, dt_raw, z, conv_w, conv_b, dt_bias, a_log, d_skip, ssd_norm)


def _out_ffn2_kernel(final, h_ref, o_ref, y_ref, woo_ref, woy_ref, nw_ref, wg_ref, wu_ref, wd_ref,
                     fw_ref, out_ref):
    h2 = h_ref[...] + _dot(o_ref[...], woo_ref[...]) + _dot(y_ref[...], woy_ref[...])
    xn = _rms(h2, nw_ref[...]).astype(bf16)
    h3 = h2 + 0.5 * _swiglu(xn, wg_ref, wu_ref, wd_ref)
    out_ref[...] = _rms(h3, fw_ref[...]) if final else h3


def _out_ffn2(h, o, y, woo, woy, nw, wg, wu, wd, fw, final):
    t, d = h.shape
    tm = TOKEN_TILE
    row = lambda i: (i, 0)
    return pl.pallas_call(
        functools.partial(_out_ffn2_kernel, final),
        grid=(t // tm,),
        in_specs=[pl.BlockSpec((tm, d), row), pl.BlockSpec((tm, o.shape[1]), row),
                  pl.BlockSpec((tm, y.shape[1]), row),
                  _const_spec(woo.shape), _const_spec(woy.shape), _const_spec(nw.shape),
                  _const_spec(wg.shape), _const_spec(wu.shape), _const_spec(wd.shape),
                  _const_spec(fw.shape)],
        out_specs=pl.BlockSpec((tm, d), row),
        out_shape=jax.ShapeDtypeStruct((t, d), f32),
        compiler_params=pltpu.CompilerParams(
            dimension_semantics=("arbitrary",), vmem_limit_bytes=VMEM_LIMIT_BYTES),
        name="out_ffn2",
    )(h, o, y, woo, woy, nw, wg, wu, wd, fw)


def kernel(x, positions, ffn1_norm, ffn1_w_gate, ffn1_w_up, ffn1_w_down, mix_norm, w_in, lambda_q1, lambda_k1, lambda_q2, lambda_k2, attn_subln, conv_w, conv_b, dt_bias, a_log, d_skip, ssd_norm, w_out, ffn2_norm, ffn2_w_gate, ffn2_w_up, ffn2_w_down, final_norm):
    batch, seq, d = x.shape
    depth = ffn1_norm.shape[0]
    t = batch * seq
    qk_cols = N_ATTN_HEADS * 2 * ATTN_HEAD_DIM
    attn_width = N_ATTN_HEADS * ATTN_V_DIM
    ssd_width = SSD_HEADS * SSD_HEAD_DIM
    conv_ch = ssd_width + 2 * SSD_GROUPS * SSD_STATE
    splits = [0, qk_cols, 2 * qk_cols, 2 * qk_cols + attn_width, 2 * qk_cols + attn_width + ssd_width,
              2 * qk_cols + attn_width + ssd_width + conv_ch]

    inv = 1.0 / (ROPE_THETA ** (jnp.arange(0, ATTN_HEAD_DIM, 2, dtype=f32) / ATTN_HEAD_DIM))
    invf = jnp.tile(inv, LANES // inv.shape[0])[None, :]
    pos = positions.reshape(t, 1)
    pad_heads = lambda a: jnp.pad(a.astype(f32), (0, LANES - SSD_HEADS))[None, :]
    row2d = lambda a: a.astype(f32)[None, :]

    h = x.reshape(t, d)
    for layer in range(depth):
        lambda_init = 0.8 - 0.6 * math.exp(-0.3 * layer)
        last = layer == depth - 1
        h1 = _ffn1(h, row2d(ffn1_norm[layer]), ffn1_w_gate[layer].astype(bf16),
                   ffn1_w_up[layer].astype(bf16), ffn1_w_down[layer].astype(bf16))

        w = w_in[layer]
        wq, wk, wv, wz, wx = (w[:, splits[i]:splits[i + 1]].astype(bf16) for i in range(5))
        wdt = jnp.pad(w[:, splits[5]:], ((0, 0), (0, LANES - SSD_HEADS))).astype(bf16)
        q, k, vt, z, xbc, dt_raw = _in_proj(h1, pos, invf, row2d(mix_norm[layer]),
                                            wq, wk, wv.T, wz, wx, wdt)

        lam = _lam(row2d(lambda_q1[layer]), row2d(lambda_k1[layer]), row2d(lambda_q2[layer]),
                   row2d(lambda_k2[layer]), lambda_init)
        o = _attention(lam, q, k, vt, row2d(attn_subln[layer]), batch, seq, 1.0 - lambda_init)

        y = _ssd(xbc, dt_raw, z, conv_w[layer].astype(f32), row2d(conv_b[layer]),
                 pad_heads(dt_bias[layer]), pad_heads(a_log[layer]),
                 row2d(jnp.repeat(d_skip[layer], SSD_HEAD_DIM)), row2d(ssd_norm[layer]), batch, seq)

        wo = w_out[layer].astype(bf16)
        h = _out_ffn2(h1, o, y, wo[:attn_width], wo[attn_width:], row2d(ffn2_norm[layer]),
                      ffn2_w_gate[layer].astype(bf16), ffn2_w_up[layer].astype(bf16),
                      ffn2_w_down[layer].astype(bf16), row2d(final_norm), final=last)
    return h.reshape(batch, seq, d)
```

```python
import functools
import math

import jax
import jax.numpy as jnp
from jax import lax
from jax.experimental import pallas as pl
from jax.experimental.pallas import tpu as pltpu

NORM_EPS = 1e-6
ROPE_THETA = 10000.0
N_ATTN_HEADS = 8
ATTN_HEAD_DIM = 64
ATTN_V_DIM = 2 * ATTN_HEAD_DIM
SSD_HEADS = 16
SSD_HEAD_DIM = 64
SSD_GROUPS = 4
SSD_HEADS_PER_GROUP = SSD_HEADS // SSD_GROUPS
SSD_STATE = 128
SSD_CONV = 4
SSD_CHUNK = 128
SSD_GROUP_WIDTH = SSD_HEADS_PER_GROUP * SSD_HEAD_DIM

LANES = 128
SUBLANES = 8
VMEM_LIMIT_BYTES = 56 * 1024 * 1024

TOKEN_TILE = 512
ATTN_BLOCK = 512
MASK_VALUE = -1e30

_NT = (((1,), (1,)), ((), ()))

bf16 = jnp.bfloat16
f32 = jnp.float32


def _rms(x, w):
    return x * lax.rsqrt(jnp.mean(x * x, axis=-1, keepdims=True) + NORM_EPS) * w


def _silu(x):
    return x * jax.nn.sigmoid(x)


def _dot(a, b):
    return jnp.dot(a, b, preferred_element_type=f32)


def _const_spec(shape):
    zeros = (0,) * len(shape)
    return pl.BlockSpec(shape, lambda *_: zeros, pipeline_mode=pl.Buffered(1))


def _swiglu(xn, wg_ref, wu_ref, wd_ref):
    g = _dot(xn, wg_ref[...])
    u = _dot(xn, wu_ref[...])
    a = (_silu(g) * u).astype(bf16)
    return _dot(a, wd_ref[...])


def _ffn1_kernel(x_ref, nw_ref, wg_ref, wu_ref, wd_ref, o_ref):
    x = x_ref[...]
    xn = _rms(x, nw_ref[...]).astype(bf16)
    o_ref[...] = x + 0.5 * _swiglu(xn, wg_ref, wu_ref, wd_ref)


def _ffn1(x, nw, wg, wu, wd):
    t, d = x.shape
    tm = TOKEN_TILE
    return pl.pallas_call(
        _ffn1_kernel,
        grid=(t // tm,),
        in_specs=[pl.BlockSpec((tm, d), lambda i: (i, 0)),
                  _const_spec(nw.shape), _const_spec(wg.shape), _const_spec(wu.shape),
                  _const_spec(wd.shape)],
        out_specs=pl.BlockSpec((tm, d), lambda i: (i, 0)),
        out_shape=jax.ShapeDtypeStruct((t, d), f32),
        compiler_params=pltpu.CompilerParams(
            dimension_semantics=("arbitrary",), vmem_limit_bytes=VMEM_LIMIT_BYTES),
        name="ffn1",
    )(x, nw, wg, wu, wd)


def _in_proj_kernel(h_ref, pos_ref, invf_ref, nw_ref, wq_ref, wk_ref, wvt_ref, wz_ref, wx_ref,
                    wdt_ref, q_ref, k_ref, vt_ref, z_ref, x_ref, dt_ref):
    u = _rms(h_ref[...], nw_ref[...]).astype(bf16)
    ang = pos_ref[...].astype(f32) * invf_ref[...]
    cos = jnp.cos(ang)
    sin = jnp.sin(ang)
    lane = lax.broadcasted_iota(jnp.int32, ang.shape, 1)
    first_half = (lane % ATTN_HEAD_DIM) < (ATTN_HEAD_DIM // 2)
    sin_signed = jnp.where(first_half, -sin, sin)

    def rope(t):
        rot = jnp.where(first_half,
                        pltpu.roll(t, LANES - ATTN_HEAD_DIM // 2, axis=1),
                        pltpu.roll(t, ATTN_HEAD_DIM // 2, axis=1))
        return t * cos + rot * sin_signed

    q = _dot(u, wq_ref[...])
    k = _dot(u, wk_ref[...])
    q_scale = ATTN_HEAD_DIM ** -0.5 * math.log2(math.e)
    for h in range(N_ATTN_HEADS):
        sl = slice(h * LANES, (h + 1) * LANES)
        q_ref[:, sl] = (rope(q[:, sl]) * q_scale).astype(bf16)
        k_ref[:, sl] = rope(k[:, sl]).astype(bf16)
    for j in range(vt_ref.shape[0]):
        ub = u[j * ATTN_BLOCK:(j + 1) * ATTN_BLOCK, :]
        vt_ref[j] = lax.dot_general(wvt_ref[...], ub, _NT, preferred_element_type=f32).astype(bf16)
    z_ref[...] = _dot(u, wz_ref[...]).astype(bf16)
    x_ref[...] = _dot(u, wx_ref[...])
    dt_ref[...] = _dot(u, wdt_ref[...])


def _in_proj(h, pos, invf, nw, wq, wk, wvt, wz, wx, wdt):
    t, d = h.shape
    tm = TOKEN_TILE
    kv_per_tile = tm // ATTN_BLOCK
    row = lambda i: (i, 0)

    def row_out(n, dtype):
        return pl.BlockSpec((tm, n), row), jax.ShapeDtypeStruct((t, n), dtype)

    outs = [row_out(wq.shape[1], bf16), row_out(wk.shape[1], bf16),
            (pl.BlockSpec((kv_per_tile, wvt.shape[0], ATTN_BLOCK), lambda i: (i, 0, 0)),
             jax.ShapeDtypeStruct((t // ATTN_BLOCK, wvt.shape[0], ATTN_BLOCK), bf16)),
            row_out(wz.shape[1], bf16), row_out(wx.shape[1], f32), row_out(wdt.shape[1], f32)]
    return pl.pallas_call(
        _in_proj_kernel,
        grid=(t // tm,),
        in_specs=[pl.BlockSpec((tm, d), row), pl.BlockSpec((tm, 1), row),
                  _const_spec(invf.shape), _const_spec(nw.shape), _const_spec(wq.shape),
                  _const_spec(wk.shape), _const_spec(wvt.shape), _const_spec(wz.shape),
                  _const_spec(wx.shape), _const_spec(wdt.shape)],
        out_specs=[spec for spec, _ in outs],
        out_shape=[shape for _, shape in outs],
        compiler_params=pltpu.CompilerParams(
            dimension_semantics=("arbitrary",), vmem_limit_bytes=VMEM_LIMIT_BYTES),
        name="in_proj",
    )(h, pos, invf, nw, wq, wk, wvt, wz, wx, wdt)


def _lam_kernel(lambda_init, q1_ref, k1_ref, q2_ref, k2_ref, o_ref):
    s1 = jnp.sum(q1_ref[...] * k1_ref[...], axis=-1, keepdims=True)
    s2 = jnp.sum(q2_ref[...] * k2_ref[...], axis=-1, keepdims=True)
    o_ref[...] = jnp.broadcast_to(jnp.exp(s1) - jnp.exp(s2) + lambda_init, o_ref.shape)


def _lam(lq1, lk1, lq2, lk2, lambda_init):
    return pl.pallas_call(
        functools.partial(_lam_kernel, lambda_init),
        out_shape=jax.ShapeDtypeStruct((1, LANES), f32),
        name="lam",
    )(lq1, lk1, lq2, lk2)


def _attn_kernel(out_scale, n_qblocks, lam_ref, q_ref, qn_ref, k_ref, vt_ref, sw_ref, o_ref,
                 qm_ref, qmn_ref, sa_ref, sb_ref, m_ref, l_ref, acc_ref):
    qi = pl.program_id(2)
    blk = ATTN_BLOCK
    half = blk // 2

    def mask_q(src_ref, dst_ref):
        q = src_ref[...]
        lane = lax.broadcasted_iota(jnp.int32, q.shape, 1)
        zero = jnp.zeros_like(q)
        dst_ref[0] = jnp.where(lane < ATTN_HEAD_DIM, q, zero)
        dst_ref[1] = jnp.where(lane < ATTN_HEAD_DIM, zero, q)

    def start():
        mask_q(q_ref, qm_ref)
        m_ref[...] = jnp.full(m_ref.shape, MASK_VALUE, f32)
        l_ref[...] = jnp.zeros(l_ref.shape, f32)
        acc_ref[...] = jnp.zeros(acc_ref.shape, f32)

    def scores(j, dst_ref, queries_ref=qm_ref):
        kb = k_ref[j * blk:(j + 1) * blk, :]
        for c in range(2):
            dst_ref[c] = lax.dot_general(kb, queries_ref[c], _NT, preferred_element_type=f32)

    def scores_for_next_step(dst_ref):
        mask_q(qn_ref, qmn_ref)
        scores(0, dst_ref, qmn_ref)

    def update(c, hq, s, vtb):
        qs = slice(hq * half, (hq + 1) * half)
        m_old = m_ref[c, :, qs]
        m_new = jnp.maximum(m_old, jnp.max(s, axis=0, keepdims=True))
        alpha = jnp.exp2(m_old - m_new)
        p = jnp.exp2(s - m_new)
        l_ref[c, :, qs] = alpha * l_ref[c, :, qs] + jnp.sum(p, axis=0, keepdims=True)
        acc_ref[c, :, qs] = alpha * acc_ref[c, :, qs] + _dot(vtb, p.astype(bf16))
        m_ref[c, :, qs] = m_new

    def full_block(j, src_ref):
        vtb = vt_ref[j]
        for c in range(2):
            for hq in range(2):
                update(c, hq, src_ref[c, :, hq * half:(hq + 1) * half], vtb)

    def diagonal_block(j, src_ref):
        vtb = vt_ref[j]
        key = lax.broadcasted_iota(jnp.int32, (half, half), 0)
        qry = lax.broadcasted_iota(jnp.int32, (half, half), 1)
        tri = key <= qry
        for c in range(2):
            update(c, 0, jnp.where(tri, src_ref[c, 0:half, 0:half], MASK_VALUE), vtb[:, 0:half])
            s_lo = jnp.where(tri, src_ref[c, half:blk, half:blk], MASK_VALUE)
            update(c, 1, jnp.concatenate([src_ref[c, 0:half, half:blk], s_lo], axis=0), vtb)

    def finish():
        lam = lam_ref[:, 0:1]
        o_t = acc_ref[0] * (1.0 / l_ref[0]) - lam * (acc_ref[1] * (1.0 / l_ref[1]))
        o_ref[...] = (_rms(o_t.T, sw_ref[...]) * out_scale).astype(bf16)

    bufs = (sa_ref, sb_ref)
    for qb in range(n_qblocks):
        first = (qb * (qb + 1) // 2) % 2

        @pl.when(qi == qb)
        def _(qb=qb, first=first):
            start()
            if qb == 0:
                scores(0, bufs[first])
            for j in range(qb):
                scores(j + 1, bufs[(first + j + 1) % 2])
                full_block(j, bufs[(first + j) % 2])
            if qb + 1 < n_qblocks:
                scores_for_next_step(bufs[(first + qb + 1) % 2])
            diagonal_block(qb, bufs[(first + qb) % 2])
            finish()


def _attention(lam, q, k, vt, subln, batch, seq, out_scale):
    t = q.shape[0]
    blk = ATTN_BLOCK
    nq = seq // blk
    return pl.pallas_call(
        functools.partial(_attn_kernel, out_scale, nq),
        grid=(batch, N_ATTN_HEADS, nq),
        in_specs=[pl.BlockSpec((1, LANES), lambda b, h, i: (0, 0)),
                  pl.BlockSpec((blk, LANES), lambda b, h, i: (b * nq + i, h)),
                  pl.BlockSpec((blk, LANES), lambda b, h, i: (b * nq + jnp.minimum(i + 1, nq - 1), h)),
                  pl.BlockSpec((seq, LANES), lambda b, h, i: (b, h)),
                  pl.BlockSpec((nq, ATTN_V_DIM, blk), lambda b, h, i: (b, h, 0)),
                  pl.BlockSpec((1, ATTN_V_DIM), lambda b, h, i: (0, 0))],
        out_specs=pl.BlockSpec((blk, ATTN_V_DIM), lambda b, h, i: (b * nq + i, h)),
        out_shape=jax.ShapeDtypeStruct((t, N_ATTN_HEADS * ATTN_V_DIM), bf16),
        scratch_shapes=[pltpu.VMEM((2, blk, LANES), bf16), pltpu.VMEM((2, blk, LANES), bf16),
                        pltpu.VMEM((2, blk, blk), f32), pltpu.VMEM((2, blk, blk), f32),
                        pltpu.VMEM((2, 1, blk), f32), pltpu.VMEM((2, 1, blk), f32),
                        pltpu.VMEM((2, ATTN_V_DIM, blk), f32)],
        compiler_params=pltpu.CompilerParams(
            dimension_semantics=("arbitrary", "arbitrary", "arbitrary")),
        name="diff_attn",
    )(lam, q, q, k, vt, subln)


def _split_bf16(x, parts):
    out = []
    for _ in range(parts - 1):
        hi = x.astype(bf16)
        out.append(hi)
        x = x - hi.astype(f32)
    out.append(x.astype(bf16))
    return out


def _ssd_kernel(xbc_ref, dt_ref, z_ref, cw_ref, cb_ref, dtb_ref, alog_ref, dskip_ref, nw_ref,
                o_ref, xext_ref, state_ref):
    c = pl.program_id(1)
    L = SSD_CHUNK
    width = SSD_HEADS * SSD_HEAD_DIM
    bc_width = SSD_GROUPS * SSD_STATE

    @pl.when(c == 0)
    def _():
        xext_ref[0:SUBLANES, :] = jnp.zeros((SUBLANES, xext_ref.shape[1]), f32)
        state_ref[...] = jnp.zeros(state_ref.shape, f32)

    @pl.when(c > 0)
    def _():
        xext_ref[0:SUBLANES, :] = xext_ref[L:L + SUBLANES, :]

    xext_ref[SUBLANES:SUBLANES + L, :] = xbc_ref[...]

    xext = xext_ref[...]
    conv = cb_ref[...] + cw_ref[SSD_CONV - 1:SSD_CONV, :] * xext[SUBLANES:, :]
    for back in range(1, SSD_CONV):
        shifted = pltpu.roll(xext, back, axis=0)[SUBLANES:, :]
        conv = conv + cw_ref[SSD_CONV - 1 - back:SSD_CONV - back, :] * shifted
    act = _silu(conv)
    xs = act[:, :width]
    bm = act[:, width:width + bc_width]
    cm = act[:, width + bc_width:]

    dt = jax.nn.softplus(dt_ref[...] + dtb_ref[...])
    d_a = dt * (-jnp.exp(alog_ref[...]))
    row = lax.broadcasted_iota(jnp.int32, (L, L), 0)
    col = lax.broadcasted_iota(jnp.int32, (L, L), 1)
    tril = row >= col
    tril_b = tril.astype(bf16)
    a_cum = sum(_dot(tril_b, part) for part in _split_bf16(d_a, 3))
    a_cum_t = a_cum.T
    a_end = a_cum[L - 1:L, :]
    decay_to_end = jnp.exp(a_end - a_cum)
    exp_a = jnp.exp(a_cum)

    hrow = lax.broadcasted_iota(jnp.int32, (LANES, width), 0)
    hcol = lax.broadcasted_iota(jnp.int32, (LANES, width), 1)
    expand = (hcol // SSD_HEAD_DIM == hrow).astype(bf16)
    dt_x = _dot(dt.astype(bf16), expand)
    dte_x = _dot(decay_to_end.astype(bf16), expand)
    ea_hi, ea_lo = _split_bf16(exp_a, 2)
    ea_x = _dot(ea_hi, expand) + _dot(ea_lo, expand)

    x_dt = xs * dt_x
    x_dt_b = x_dt.astype(bf16)
    x_end_b = (x_dt * dte_x).astype(bf16)
    gcol = lax.broadcasted_iota(jnp.int32, (L, SSD_GROUP_WIDTH), 1) // SSD_HEAD_DIM
    zero_g = jnp.zeros((L, SSD_GROUP_WIDTH), bf16)

    y = dskip_ref[...] * xs
    y_parts = []
    for g in range(SSD_GROUPS):
        gs = slice(g * SSD_GROUP_WIDTH, (g + 1) * SSD_GROUP_WIDTH)
        ns = slice(g * SSD_STATE, (g + 1) * SSD_STATE)
        b_g = bm[:, ns]
        c_g = cm[:, ns].astype(bf16)
        cb = lax.dot_general(c_g, b_g.astype(bf16), _NT, preferred_element_type=f32)
        b_t = b_g.T.astype(bf16)
        st = state_ref[g]
        y_g = _dot(c_g, st.astype(bf16)) * ea_x[:, gs]
        state_ref[g] = st * ea_x[L - 1:L, gs] + _dot(b_t, x_end_b[:, gs])
        x_g = x_dt_b[:, gs]
        for r in range(SSD_HEADS_PER_GROUP):
            h = g * SSD_HEADS_PER_GROUP + r
            seg = a_cum[:, h:h + 1] - a_cum_t[h:h + 1, :]
            decay = jnp.exp(jnp.where(tril, seg, MASK_VALUE))
            m = (cb * decay).astype(bf16)
            y_g = y_g + _dot(m, jnp.where(gcol == r, x_g, zero_g))
        y_parts.append(y_g)

    for g in range(SSD_GROUPS):
        gs = slice(g * SSD_GROUP_WIDTH, (g + 1) * SSD_GROUP_WIDTH)
        y_g = (y[:, gs] + y_parts[g]) * _silu(z_ref[:, gs].astype(f32))
        o_ref[:, gs] = _rms(y_g, nw_ref[:, gs]).astype(bf16)


def _ssd(xbc, dt_raw, z, conv_w, conv_b, dt_bias, a_log, d_skip, ssd_norm, batch, seq):
    t, ch = xbc.shape
    L = SSD_CHUNK
    nc = seq // L
    width = SSD_HEADS * SSD_HEAD_DIM
    row = lambda b, c: (b * nc + c, 0)
    return pl.pallas_call(
        _ssd_kernel,
        grid=(batch, nc),
        in_specs=[pl.BlockSpec((L, ch), row), pl.BlockSpec((L, LANES), row),
                  pl.BlockSpec((L, width), row),
                  _const_spec(conv_w.shape), _const_spec(conv_b.shape), _const_spec(dt_bias.shape),
                  _const_spec(a_log.shape), _const_spec(d_skip.shape), _const_spec(ssd_norm.shape)],
        out_specs=pl.BlockSpec((L, width), row),
        out_shape=jax.ShapeDtypeStruct((t, width), bf16),
        scratch_shapes=[pltpu.VMEM((L + SUBLANES, ch), f32),
                        pltpu.VMEM((SSD_GROUPS, SSD_STATE, SSD_GROUP_WIDTH), f32)],
        compiler_params=pltpu.CompilerParams(dimension_semantics=("arbitrary", "arbitrary")),
        name="ssd",
    )(xbc, dt_raw, z, conv_w, conv_b, dt_bias, a_log, d_skip, ssd_norm)


def _out_ffn2_kernel(final, h_ref, o_ref, y_ref, woo_ref, woy_ref, nw_ref, wg_ref, wu_ref, wd_ref,
                     fw_ref, out_ref):
    h2 = h_ref[...] + _dot(o_ref[...], woo_ref[...]) + _dot(y_ref[...], woy_ref[...])
    xn = _rms(h2, nw_ref[...]).astype(bf16)
    h3 = h2 + 0.5 * _swiglu(xn, wg_ref, wu_ref, wd_ref)
    out_ref[...] = _rms(h3, fw_ref[...]) if final else h3


def _out_ffn2(h, o, y, woo, woy, nw, wg, wu, wd, fw, final):
    t, d = h.shape
    tm = TOKEN_TILE
    row = lambda i: (i, 0)
    return pl.pallas_call(
        functools.partial(_out_ffn2_kernel, final),
        grid=(t // tm,),
        in_specs=[pl.BlockSpec((tm, d), row), pl.BlockSpec((tm, o.shape[1]), row),
                  pl.BlockSpec((tm, y.shape[1]), row),
                  _const_spec(woo.shape), _const_spec(woy.shape), _const_spec(nw.shape),
                  _const_spec(wg.shape), _const_spec(wu.shape), _const_spec(wd.shape),
                  _const_spec(fw.shape)],
        out_specs=pl.BlockSpec((tm, d), row),
        out_shape=jax.ShapeDtypeStruct((t, d), f32),
        compiler_params=pltpu.CompilerParams(
            dimension_semantics=("arbitrary",), vmem_limit_bytes=VMEM_LIMIT_BYTES),
        name="out_ffn2",
    )(h, o, y, woo, woy, nw, wg, wu, wd, fw)


def kernel(x, positions, ffn1_norm, ffn1_w_gate, ffn1_w_up, ffn1_w_down, mix_norm, w_in, lambda_q1, lambda_k1, lambda_q2, lambda_k2, attn_subln, conv_w, conv_b, dt_bias, a_log, d_skip, ssd_norm, w_out, ffn2_norm, ffn2_w_gate, ffn2_w_up, ffn2_w_down, final_norm):
    batch, seq, d = x.shape
    depth = ffn1_norm.shape[0]
    t = batch * seq
    qk_cols = N_ATTN_HEADS * 2 * ATTN_HEAD_DIM
    attn_width = N_ATTN_HEADS * ATTN_V_DIM
    ssd_width = SSD_HEADS * SSD_HEAD_DIM
    conv_ch = ssd_width + 2 * SSD_GROUPS * SSD_STATE
    splits = [0, qk_cols, 2 * qk_cols, 2 * qk_cols + attn_width, 2 * qk_cols + attn_width + ssd_width,
              2 * qk_cols + attn_width + ssd_width + conv_ch]

    inv = 1.0 / (ROPE_THETA ** (jnp.arange(0, ATTN_HEAD_DIM, 2, dtype=f32) / ATTN_HEAD_DIM))
    invf = jnp.tile(inv, LANES // inv.shape[0])[None, :]
    pos = positions.reshape(t, 1)
    pad_heads = lambda a: jnp.pad(a.astype(f32), (0, LANES - SSD_HEADS))[None, :]
    row2d = lambda a: a.astype(f32)[None, :]

    h = x.reshape(t, d)
    for layer in range(depth):
        lambda_init = 0.8 - 0.6 * math.exp(-0.3 * layer)
        last = layer == depth - 1
        h1 = _ffn1(h, row2d(ffn1_norm[layer]), ffn1_w_gate[layer].astype(bf16),
                   ffn1_w_up[layer].astype(bf16), ffn1_w_down[layer].astype(bf16))

        w = w_in[layer]
        wq, wk, wv, wz, wx = (w[:, splits[i]:splits[i + 1]].astype(bf16) for i in range(5))
        wdt = jnp.pad(w[:, splits[5]:], ((0, 0), (0, LANES - SSD_HEADS))).astype(bf16)
        q, k, vt, z, xbc, dt_raw = _in_proj(h1, pos, invf, row2d(mix_norm[layer]),
                                            wq, wk, wv.T, wz, wx, wdt)

        lam = _lam(row2d(lambda_q1[layer]), row2d(lambda_k1[layer]), row2d(lambda_q2[layer]),
                   row2d(lambda_k2[layer]), lambda_init)
        o = _attention(lam, q, k, vt, row2d(attn_subln[layer]), batch, seq, 1.0 - lambda_init)

        y = _ssd(xbc, dt_raw, z, conv_w[layer].astype(f32), row2d(conv_b[layer]),
                 pad_heads(dt_bias[layer]), pad_heads(a_log[layer]),
                 row2d(jnp.repeat(d_skip[layer], SSD_HEAD_DIM)), row2d(ssd_norm[layer]), batch, seq)

        wo = w_out[layer].astype(bf16)
        h = _out_ffn2(h1, o, y, wo[:attn_width], wo[attn_width:], row2d(ffn2_norm[layer]),
                      ffn2_w_gate[layer].astype(bf16), ffn2_w_up[layer].astype(bf16),
                      ffn2_w_down[layer].astype(bf16), row2d(final_norm), final=last)
    return h.reshape(batch, seq, d)
```

```python
import functools
import math

import jax
import jax.numpy as jnp
from jax import lax
from jax.experimental import pallas as pl
from jax.experimental.pallas import tpu as pltpu

NORM_EPS = 1e-6
ROPE_THETA = 10000.0
N_ATTN_HEADS = 8
ATTN_HEAD_DIM = 64
ATTN_V_DIM = 2 * ATTN_HEAD_DIM
SSD_HEADS = 16
SSD_HEAD_DIM = 64
SSD_GROUPS = 4
SSD_HEADS_PER_GROUP = SSD_HEADS // SSD_GROUPS
SSD_STATE = 128
SSD_CONV = 4
SSD_CHUNK = 128
SSD_CHUNKS_PER_STEP = 4
SSD_GROUP_WIDTH = SSD_HEADS_PER_GROUP * SSD_HEAD_DIM

LANES = 128
SUBLANES = 8
VMEM_LIMIT_BYTES = 56 * 1024 * 1024

TOKEN_TILE = 512
ATTN_BLOCK = 512
MASK_VALUE = -1e30

_NT = (((1,), (1,)), ((), ()))

bf16 = jnp.bfloat16
f32 = jnp.float32


def _rms(x, w):
    return x * lax.rsqrt(jnp.mean(x * x, axis=-1, keepdims=True) + NORM_EPS) * w


def _silu(x):
    return x * jax.nn.sigmoid(x)


def _dot(a, b):
    return jnp.dot(a, b, preferred_element_type=f32)


def _const_spec(shape):
    zeros = (0,) * len(shape)
    return pl.BlockSpec(shape, lambda *_: zeros, pipeline_mode=pl.Buffered(1))


def _swiglu(xn, wg_ref, wu_ref, wd_ref):
    g = _dot(xn, wg_ref[...])
    u = _dot(xn, wu_ref[...])
    a = (_silu(g) * u).astype(bf16)
    return _dot(a, wd_ref[...])


def _ffn1_kernel(x_ref, nw_ref, wg_ref, wu_ref, wd_ref, o_ref):
    x = x_ref[...]
    xn = _rms(x, nw_ref[...]).astype(bf16)
    o_ref[...] = x + 0.5 * _swiglu(xn, wg_ref, wu_ref, wd_ref)


def _ffn1(x, nw, wg, wu, wd):
    t, d = x.shape
    tm = TOKEN_TILE
    return pl.pallas_call(
        _ffn1_kernel,
        grid=(t // tm,),
        in_specs=[pl.BlockSpec((tm, d), lambda i: (i, 0)),
                  _const_spec(nw.shape), _const_spec(wg.shape), _const_spec(wu.shape),
                  _const_spec(wd.shape)],
        out_specs=pl.BlockSpec((tm, d), lambda i: (i, 0)),
        out_shape=jax.ShapeDtypeStruct((t, d), f32),
        compiler_params=pltpu.CompilerParams(
            dimension_semantics=("arbitrary",), vmem_limit_bytes=VMEM_LIMIT_BYTES),
        name="ffn1",
    )(x, nw, wg, wu, wd)


def _in_proj_kernel(h_ref, pos_ref, invf_ref, nw_ref, wq_ref, wk_ref, wvt_ref, wz_ref, wx_ref,
                    wdt_ref, q_ref, k_ref, vt_ref, z_ref, x_ref, dt_ref):
    u = _rms(h_ref[...], nw_ref[...]).astype(bf16)
    ang = pos_ref[...].astype(f32) * invf_ref[...]
    cos = jnp.cos(ang)
    sin = jnp.sin(ang)
    lane = lax.broadcasted_iota(jnp.int32, ang.shape, 1)
    first_half = (lane % ATTN_HEAD_DIM) < (ATTN_HEAD_DIM // 2)
    sin_signed = jnp.where(first_half, -sin, sin)

    def rope(t):
        rot = jnp.where(first_half,
                        pltpu.roll(t, LANES - ATTN_HEAD_DIM // 2, axis=1),
                        pltpu.roll(t, ATTN_HEAD_DIM // 2, axis=1))
        return t * cos + rot * sin_signed

    q = _dot(u, wq_ref[...])
    k = _dot(u, wk_ref[...])
    q_scale = ATTN_HEAD_DIM ** -0.5 * math.log2(math.e)
    for h in range(N_ATTN_HEADS):
        sl = slice(h * LANES, (h + 1) * LANES)
        q_ref[:, sl] = (rope(q[:, sl]) * q_scale).astype(bf16)
        k_ref[:, sl] = rope(k[:, sl]).astype(bf16)
    for j in range(vt_ref.shape[0]):
        ub = u[j * ATTN_BLOCK:(j + 1) * ATTN_BLOCK, :]
        vt_ref[j] = lax.dot_general(wvt_ref[...], ub, _NT, preferred_element_type=f32).astype(bf16)
    z_ref[...] = _dot(u, wz_ref[...]).astype(bf16)
    x_ref[...] = _dot(u, wx_ref[...])
    dt_ref[...] = _dot(u, wdt_ref[...])


def _in_proj(h, pos, invf, nw, wq, wk, wvt, wz, wx, wdt):
    t, d = h.shape
    tm = TOKEN_TILE
    kv_per_tile = tm // ATTN_BLOCK
    row = lambda i: (i, 0)

    def row_out(n, dtype):
        return pl.BlockSpec((tm, n), row), jax.ShapeDtypeStruct((t, n), dtype)

    outs = [row_out(wq.shape[1], bf16), row_out(wk.shape[1], bf16),
            (pl.BlockSpec((kv_per_tile, wvt.shape[0], ATTN_BLOCK), lambda i: (i, 0, 0)),
             jax.ShapeDtypeStruct((t // ATTN_BLOCK, wvt.shape[0], ATTN_BLOCK), bf16)),
            row_out(wz.shape[1], bf16), row_out(wx.shape[1], f32), row_out(wdt.shape[1], f32)]
    return pl.pallas_call(
        _in_proj_kernel,
        grid=(t // tm,),
        in_specs=[pl.BlockSpec((tm, d), row), pl.BlockSpec((tm, 1), row),
                  _const_spec(invf.shape), _const_spec(nw.shape), _const_spec(wq.shape),
                  _const_spec(wk.shape), _const_spec(wvt.shape), _const_spec(wz.shape),
                  _const_spec(wx.shape), _const_spec(wdt.shape)],
        out_specs=[spec for spec, _ in outs],
        out_shape=[shape for _, shape in outs],
        compiler_params=pltpu.CompilerParams(
            dimension_semantics=("arbitrary",), vmem_limit_bytes=VMEM_LIMIT_BYTES),
        name="in_proj",
    )(h, pos, invf, nw, wq, wk, wvt, wz, wx, wdt)


def _lam_kernel(lambda_init, q1_ref, k1_ref, q2_ref, k2_ref, o_ref):
    s1 = jnp.sum(q1_ref[...] * k1_ref[...], axis=-1, keepdims=True)
    s2 = jnp.sum(q2_ref[...] * k2_ref[...], axis=-1, keepdims=True)
    o_ref[...] = jnp.broadcast_to(jnp.exp(s1) - jnp.exp(s2) + lambda_init, o_ref.shape)


def _lam(lq1, lk1, lq2, lk2, lambda_init):
    return pl.pallas_call(
        functools.partial(_lam_kernel, lambda_init),
        out_shape=jax.ShapeDtypeStruct((1, LANES), f32),
        name="lam",
    )(lq1, lk1, lq2, lk2)


def _attn_kernel(out_scale, n_qblocks, lam_ref, q_ref, qn_ref, k_ref, vt_ref, sw_ref, o_ref,
                 qm_ref, qmn_ref, sa_ref, sb_ref, m_ref, l_ref, acc_ref):
    qi = pl.program_id(2)
    blk = ATTN_BLOCK
    half = blk // 2

    def mask_q(src_ref, dst_ref):
        q = src_ref[...]
        lane = lax.broadcasted_iota(jnp.int32, q.shape, 1)
        zero = jnp.zeros_like(q)
        dst_ref[0] = jnp.where(lane < ATTN_HEAD_DIM, q, zero)
        dst_ref[1] = jnp.where(lane < ATTN_HEAD_DIM, zero, q)

    def start():
        mask_q(q_ref, qm_ref)
        m_ref[...] = jnp.full(m_ref.shape, MASK_VALUE, f32)
        l_ref[...] = jnp.zeros(l_ref.shape, f32)
        acc_ref[...] = jnp.zeros(acc_ref.shape, f32)

    def scores(j, dst_ref, queries_ref=qm_ref):
        kb = k_ref[j * blk:(j + 1) * blk, :]
        for c in range(2):
            dst_ref[c] = lax.dot_general(kb, queries_ref[c], _NT, preferred_element_type=f32)

    def scores_for_next_step(dst_ref):
        mask_q(qn_ref, qmn_ref)
        scores(0, dst_ref, qmn_ref)

    def update(c, hq, s, vtb):
        qs = slice(hq * half, (hq + 1) * half)
        m_old = m_ref[c, :, qs]
        m_new = jnp.maximum(m_old, jnp.max(s, axis=0, keepdims=True))
        alpha = jnp.exp2(m_old - m_new)
        p = jnp.exp2(s - m_new)
        l_ref[c, :, qs] = alpha * l_ref[c, :, qs] + jnp.sum(p, axis=0, keepdims=True)
        acc_ref[c, :, qs] = alpha * acc_ref[c, :, qs] + _dot(vtb, p.astype(bf16))
        m_ref[c, :, qs] = m_new

    def full_block(j, src_ref):
        vtb = vt_ref[j]
        for c in range(2):
            for hq in range(2):
                update(c, hq, src_ref[c, :, hq * half:(hq + 1) * half], vtb)

    def diagonal_block(j, src_ref):
        vtb = vt_ref[j]
        key = lax.broadcasted_iota(jnp.int32, (half, half), 0)
        qry = lax.broadcasted_iota(jnp.int32, (half, half), 1)
        tri = key <= qry
        for c in range(2):
            update(c, 0, jnp.where(tri, src_ref[c, 0:half, 0:half], MASK_VALUE), vtb[:, 0:half])
            s_lo = jnp.where(tri, src_ref[c, half:blk, half:blk], MASK_VALUE)
            update(c, 1, jnp.concatenate([src_ref[c, 0:half, half:blk], s_lo], axis=0), vtb)

    def finish():
        lam = lam_ref[:, 0:1]
        o_t = acc_ref[0] * (1.0 / l_ref[0]) - lam * (acc_ref[1] * (1.0 / l_ref[1]))
        o_ref[...] = (_rms(o_t.T, sw_ref[...]) * out_scale).astype(bf16)

    bufs = (sa_ref, sb_ref)
    for qb in range(n_qblocks):
        first = (qb * (qb + 1) // 2) % 2

        @pl.when(qi == qb)
        def _(qb=qb, first=first):
            start()
            if qb == 0:
                scores(0, bufs[first])
            for j in range(qb):
                scores(j + 1, bufs[(first + j + 1) % 2])
                full_block(j, bufs[(first + j) % 2])
            if qb + 1 < n_qblocks:
                scores_for_next_step(bufs[(first + qb + 1) % 2])
            diagonal_block(qb, bufs[(first + qb) % 2])
            finish()


def _attention(lam, q, k, vt, subln, batch, seq, out_scale):
    t = q.shape[0]
    blk = ATTN_BLOCK
    nq = seq // blk
    return pl.pallas_call(
        functools.partial(_attn_kernel, out_scale, nq),
        grid=(batch, N_ATTN_HEADS, nq),
        in_specs=[pl.BlockSpec((1, LANES), lambda b, h, i: (0, 0)),
                  pl.BlockSpec((blk, LANES), lambda b, h, i: (b * nq + i, h)),
                  pl.BlockSpec((blk, LANES), lambda b, h, i: (b * nq + jnp.minimum(i + 1, nq - 1), h)),
                  pl.BlockSpec((seq, LANES), lambda b, h, i: (b, h)),
                  pl.BlockSpec((nq, ATTN_V_DIM, blk), lambda b, h, i: (b, h, 0)),
                  pl.BlockSpec((1, ATTN_V_DIM), lambda b, h, i: (0, 0))],
        out_specs=pl.BlockSpec((blk, ATTN_V_DIM), lambda b, h, i: (b * nq + i, h)),
        out_shape=jax.ShapeDtypeStruct((t, N_ATTN_HEADS * ATTN_V_DIM), bf16),
        scratch_shapes=[pltpu.VMEM((2, blk, LANES), bf16), pltpu.VMEM((2, blk, LANES), bf16),
                        pltpu.VMEM((2, blk, blk), f32), pltpu.VMEM((2, blk, blk), f32),
                        pltpu.VMEM((2, 1, blk), f32), pltpu.VMEM((2, 1, blk), f32),
                        pltpu.VMEM((2, ATTN_V_DIM, blk), f32)],
        compiler_params=pltpu.CompilerParams(
            dimension_semantics=("arbitrary", "arbitrary", "arbitrary")),
        name="diff_attn",
    )(lam, q, q, k, vt, subln)


def _split_bf16(x, parts):
    out = []
    for _ in range(parts - 1):
        hi = x.astype(bf16)
        out.append(hi)
        x = x - hi.astype(f32)
    out.append(x.astype(bf16))
    return out


def _ssd_kernel(xbc_ref, dt_ref, z_ref, cw_ref, cb_ref, dtb_ref, alog_ref, dskip_ref, nw_ref,
                o_ref, tail_ref, state_ref):
    step = pl.program_id(1)
    L = SSD_CHUNK
    width = SSD_HEADS * SSD_HEAD_DIM
    bc_width = SSD_GROUPS * SSD_STATE

    @pl.when(step == 0)
    def _():
        tail_ref[...] = jnp.zeros(tail_ref.shape, f32)
        state_ref[...] = jnp.zeros(state_ref.shape, f32)

    row = lax.broadcasted_iota(jnp.int32, (L, L), 0)
    col = lax.broadcasted_iota(jnp.int32, (L, L), 1)
    tril = row >= col
    tril_b = tril.astype(bf16)
    hrow = lax.broadcasted_iota(jnp.int32, (LANES, width), 0)
    hcol = lax.broadcasted_iota(jnp.int32, (LANES, width), 1)
    expand = (hcol // SSD_HEAD_DIM == hrow).astype(bf16)
    gcol = lax.broadcasted_iota(jnp.int32, (L, SSD_GROUP_WIDTH), 1) // SSD_HEAD_DIM
    zero_g = jnp.zeros((L, SSD_GROUP_WIDTH), bf16)
    neg_a = -jnp.exp(alog_ref[...])
    w0, w1, w2, w3 = (cw_ref[k:k + 1, :] for k in range(SSD_CONV))

    for i in range(SSD_CHUNKS_PER_STEP):
        rows = slice(i * L, (i + 1) * L)
        if i == 0:
            ext = jnp.concatenate([tail_ref[...], xbc_ref[rows, :]], axis=0)
        else:
            ext = xbc_ref[i * L - SUBLANES:(i + 1) * L, :]
        ext1 = pltpu.roll(ext, 1, axis=0)
        pair_lo = w1 * ext + w0 * ext1
        pair_hi = w3 * ext[SUBLANES:, :] + w2 * ext1[SUBLANES:, :]
        conv = cb_ref[...] + pair_hi + pltpu.roll(pair_lo, 2, axis=0)[SUBLANES:, :]
        act = _silu(conv)
        xs = act[:, :width]
        bm = act[:, width:width + bc_width]
        cm = act[:, width + bc_width:]

        dt = jax.nn.softplus(dt_ref[rows, :] + dtb_ref[...])
        d_a = dt * neg_a
        a_cum = sum(_dot(tril_b, part) for part in _split_bf16(d_a, 3))
        a_cum_t = a_cum.T
        a_end = a_cum[L - 1:L, :]
        decay_to_end = jnp.exp(a_end - a_cum)
        exp_a = jnp.exp(a_cum)

        dt_x = _dot(dt.astype(bf16), expand)
        dte_x = _dot(decay_to_end.astype(bf16), expand)
        ea_hi, ea_lo = _split_bf16(exp_a, 2)
        ea_x = _dot(ea_hi, expand) + _dot(ea_lo, expand)

        x_dt = xs * dt_x
        x_dt_b = x_dt.astype(bf16)
        x_end_b = (x_dt * dte_x).astype(bf16)

        for g in range(SSD_GROUPS):
            gs = slice(g * SSD_GROUP_WIDTH, (g + 1) * SSD_GROUP_WIDTH)
            ns = slice(g * SSD_STATE, (g + 1) * SSD_STATE)
            b_g = bm[:, ns]
            c_g = cm[:, ns].astype(bf16)
            cb = lax.dot_general(c_g, b_g.astype(bf16), _NT, preferred_element_type=f32)
            b_t = b_g.T.astype(bf16)
            st = state_ref[g]
            y_g = _dot(c_g, st.astype(bf16)) * ea_x[:, gs]
            state_ref[g] = st * ea_x[L - 1:L, gs] + _dot(b_t, x_end_b[:, gs])
            x_g = x_dt_b[:, gs]
            m_heads = []
            for r in range(SSD_HEADS_PER_GROUP):
                h = g * SSD_HEADS_PER_GROUP + r
                seg = a_cum[:, h:h + 1] - a_cum_t[h:h + 1, :]
                decay = jnp.exp(jnp.where(tril, seg, MASK_VALUE))
                m_heads.append((cb * decay).astype(bf16))
            x_bd = jnp.concatenate([jnp.where(gcol == r, x_g, zero_g)
                                    for r in range(SSD_HEADS_PER_GROUP)], axis=0)
            y_g = y_g + _dot(jnp.concatenate(m_heads, axis=1), x_bd) + dskip_ref[:, gs] * xs[:, gs]
            y_g = y_g * _silu(z_ref[rows, gs].astype(f32))
            o_ref[rows, gs] = _rms(y_g, nw_ref[:, gs]).astype(bf16)

    tail_ref[...] = xbc_ref[SSD_CHUNKS_PER_STEP * L - SUBLANES:SSD_CHUNKS_PER_STEP * L, :]


def _ssd(xbc, dt_raw, z, conv_w, conv_b, dt_bias, a_log, d_skip, ssd_norm, batch, seq):
    t, ch = xbc.shape
    rows = SSD_CHUNKS_PER_STEP * SSD_CHUNK
    steps = seq // rows
    width = SSD_HEADS * SSD_HEAD_DIM
    row = lambda b, s: (b * steps + s, 0)
    return pl.pallas_call(
        _ssd_kernel,
        grid=(batch, steps),
        in_specs=[pl.BlockSpec((rows, ch), row), pl.BlockSpec((rows, LANES), row),
                  pl.BlockSpec((rows, width), row),
                  _const_spec(conv_w.shape), _const_spec(conv_b.shape), _const_spec(dt_bias.shape),
                  _const_spec(a_log.shape), _const_spec(d_skip.shape), _const_spec(ssd_norm.shape)],
        out_specs=pl.BlockSpec((rows, width), row),
        out_shape=jax.ShapeDtypeStruct((t, width), bf16),
        scratch_shapes=[pltpu.VMEM((SUBLANES, ch), f32),
                        pltpu.VMEM((SSD_GROUPS, SSD_STATE, SSD_GROUP_WIDTH), f32)],
        compiler_params=pltpu.CompilerParams(dimension_semantics=("arbitrary", "arbitrary")),
        name="ssd",
    )(xbc, dt_raw, z, conv_w, conv_b, dt_bias, a_log, d_skip, ssd_norm)


def _out_ffn2_kernel(final, h_ref, o_ref, y_ref, woo_ref, woy_ref, nw_ref, wg_ref, wu_ref, wd_ref,
                     fw_ref, out_ref):
    h2 = h_ref[...] + _dot(o_ref[...], woo_ref[...]) + _dot(y_ref[...], woy_ref[...])
    xn = _rms(h2, nw_ref[...]).astype(bf16)
    h3 = h2 + 0.5 * _swiglu(xn, wg_ref, wu_ref, wd_ref)
    out_ref[...] = _rms(h3, fw_ref[...]) if final else h3


def _out_ffn2(h, o, y, woo, woy, nw, wg, wu, wd, fw, final):
    t, d = h.shape
    tm = TOKEN_TILE
    row = lambda i: (i, 0)
    return pl.pallas_call(
        functools.partial(_out_ffn2_kernel, final),
        grid=(t // tm,),
        in_specs=[pl.BlockSpec((tm, d), row), pl.BlockSpec((tm, o.shape[1]), row),
                  pl.BlockSpec((tm, y.shape[1]), row),
                  _const_spec(woo.shape), _const_spec(woy.shape), _const_spec(nw.shape),
                  _const_spec(wg.shape), _const_spec(wu.shape), _const_spec(wd.shape),
                  _const_spec(fw.shape)],
        out_specs=pl.BlockSpec((tm, d), row),
        out_shape=jax.ShapeDtypeStruct((t, d), f32),
        compiler_params=pltpu.CompilerParams(
            dimension_semantics=("arbitrary",), vmem_limit_bytes=VMEM_LIMIT_BYTES),
        name="out_ffn2",
    )(h, o, y, woo, woy, nw, wg, wu, wd, fw)


def kernel(x, positions, ffn1_norm, ffn1_w_gate, ffn1_w_up, ffn1_w_down, mix_norm, w_in, lambda_q1, lambda_k1, lambda_q2, lambda_k2, attn_subln, conv_w, conv_b, dt_bias, a_log, d_skip, ssd_norm, w_out, ffn2_norm, ffn2_w_gate, ffn2_w_up, ffn2_w_down, final_norm):
    batch, seq, d = x.shape
    depth = ffn1_norm.shape[0]
    t = batch * seq
    qk_cols = N_ATTN_HEADS * 2 * ATTN_HEAD_DIM
    attn_width = N_ATTN_HEADS * ATTN_V_DIM
    ssd_width = SSD_HEADS * SSD_HEAD_DIM
    conv_ch = ssd_width + 2 * SSD_GROUPS * SSD_STATE
    splits = [0, qk_cols, 2 * qk_cols, 2 * qk_cols + attn_width, 2 * qk_cols + attn_width + ssd_width,
              2 * qk_cols + attn_width + ssd_width + conv_ch]

    inv = 1.0 / (ROPE_THETA ** (jnp.arange(0, ATTN_HEAD_DIM, 2, dtype=f32) / ATTN_HEAD_DIM))
    invf = jnp.tile(inv, LANES // inv.shape[0])[None, :]
    pos = positions.reshape(t, 1)
    pad_heads = lambda a: jnp.pad(a.astype(f32), (0, LANES - SSD_HEADS))[None, :]
    row2d = lambda a: a.astype(f32)[None, :]

    h = x.reshape(t, d)
    for layer in range(depth):
        lambda_init = 0.8 - 0.6 * math.exp(-0.3 * layer)
        last = layer == depth - 1
        h1 = _ffn1(h, row2d(ffn1_norm[layer]), ffn1_w_gate[layer].astype(bf16),
                   ffn1_w_up[layer].astype(bf16), ffn1_w_down[layer].astype(bf16))

        w = w_in[layer]
        wq, wk, wv, wz, wx = (w[:, splits[i]:splits[i + 1]].astype(bf16) for i in range(5))
        wdt = jnp.pad(w[:, splits[5]:], ((0, 0), (0, LANES - SSD_HEADS))).astype(bf16)
        q, k, vt, z, xbc, dt_raw = _in_proj(h1, pos, invf, row2d(mix_norm[layer]),
                                            wq, wk, wv.T, wz, wx, wdt)

        lam = _lam(row2d(lambda_q1[layer]), row2d(lambda_k1[layer]), row2d(lambda_q2[layer]),
                   row2d(lambda_k2[layer]), lambda_init)
        o = _attention(lam, q, k, vt, row2d(attn_subln[layer]), batch, seq, 1.0 - lambda_init)

        y = _ssd(xbc, dt_raw, z, conv_w[layer].astype(f32), row2d(conv_b[layer]),
                 pad_heads(dt_bias[layer]), pad_heads(a_log[layer]),
                 row2d(jnp.repeat(d_skip[layer], SSD_HEAD_DIM)), row2d(ssd_norm[layer]), batch, seq)

        wo = w_out[layer].astype(bf16)
        h = _out_ffn2(h1, o, y, wo[:attn_width], wo[attn_width:], row2d(ffn2_norm[layer]),
                      ffn2_w_gate[layer].astype(bf16), ffn2_w_up[layer].astype(bf16),
                      ffn2_w_down[layer].astype(bf16), row2d(final_norm), final=last)
    return h.reshape(batch, seq, d)
```

```python
import functools
import math

import jax
import jax.numpy as jnp
from jax import lax
from jax.experimental import pallas as pl
from jax.experimental.pallas import tpu as pltpu

NORM_EPS = 1e-6
ROPE_THETA = 10000.0
N_ATTN_HEADS = 8
ATTN_HEAD_DIM = 64
ATTN_V_DIM = 2 * ATTN_HEAD_DIM
SSD_HEADS = 16
SSD_HEAD_DIM = 64
SSD_GROUPS = 4
SSD_HEADS_PER_GROUP = SSD_HEADS // SSD_GROUPS
SSD_STATE = 128
SSD_CONV = 4
SSD_CHUNK = 128
SSD_CHUNKS_PER_STEP = 4
SSD_GROUP_WIDTH = SSD_HEADS_PER_GROUP * SSD_HEAD_DIM

LANES = 128
SUBLANES = 8
VMEM_LIMIT_BYTES = 56 * 1024 * 1024

TOKEN_TILE = 512
ATTN_BLOCK = 512
ATTN_HEADS_PER_STEP = 1
MASK_VALUE = -1e30

_NT = (((1,), (1,)), ((), ()))

bf16 = jnp.bfloat16
f32 = jnp.float32


def _inv_rms(x):
    return lax.rsqrt(jnp.mean(x * x, axis=-1, keepdims=True) + NORM_EPS)


def _rms(x, w):
    return x * _inv_rms(x) * w


def _silu(x):
    return x * jax.nn.sigmoid(x)


def _dot(a, b):
    return jnp.dot(a, b, preferred_element_type=f32)


def _const_spec(shape):
    zeros = (0,) * len(shape)
    return pl.BlockSpec(shape, lambda *_: zeros, pipeline_mode=pl.Buffered(1))


def _swiglu_of_rms(x, nw_ref, wg_ref, wu_ref, wd_ref):
    xw = (x * nw_ref[...]).astype(bf16)
    r = _inv_rms(x)
    g = _dot(xw, wg_ref[...]) * r
    u = _dot(xw, wu_ref[...]) * r
    a = (_silu(g) * u).astype(bf16)
    return _dot(a, wd_ref[...])


def _ffn1_kernel(x_ref, nw_ref, wg_ref, wu_ref, wd_ref, o_ref):
    half = x_ref.shape[0] // 2
    for r0 in (0, half):
        x = x_ref[r0:r0 + half, :]
        o_ref[r0:r0 + half, :] = x + 0.5 * _swiglu_of_rms(x, nw_ref, wg_ref, wu_ref, wd_ref)


def _ffn1(x, nw, wg, wu, wd):
    t, d = x.shape
    tm = TOKEN_TILE
    return pl.pallas_call(
        _ffn1_kernel,
        grid=(t // tm,),
        in_specs=[pl.BlockSpec((tm, d), lambda i: (i, 0)),
                  _const_spec(nw.shape), _const_spec(wg.shape), _const_spec(wu.shape),
                  _const_spec(wd.shape)],
        out_specs=pl.BlockSpec((tm, d), lambda i: (i, 0)),
        out_shape=jax.ShapeDtypeStruct((t, d), f32),
        compiler_params=pltpu.CompilerParams(
            dimension_semantics=("arbitrary",), vmem_limit_bytes=VMEM_LIMIT_BYTES),
        name="ffn1",
    )(x, nw, wg, wu, wd)


def _in_proj_kernel(h_ref, pos_ref, invf_ref, nw_ref, wq_ref, wk_ref, wvt_ref, wz_ref, wx_ref,
                    wdt_ref, q_ref, k_ref, vt_ref, z_ref, x_ref, dt_ref):
    half = h_ref.shape[0] // 2
    assert ATTN_BLOCK % half == 0
    for r0 in (0, half):
        rows = slice(r0, r0 + half)
        h = h_ref[rows, :]
        hw = h * nw_ref[...]
        r = _inv_rms(h)
        ub = hw.astype(bf16)
        u = (hw * r).astype(bf16)

        ang = pos_ref[rows, :].astype(f32) * invf_ref[...]
        cos = jnp.cos(ang) * r
        sin = jnp.sin(ang) * r
        lane = lax.broadcasted_iota(jnp.int32, ang.shape, 1)
        sin_signed = jnp.where(lane < LANES // 2, -sin, sin)
        q_scale = ATTN_HEAD_DIM ** -0.5 * math.log2(math.e)
        cos_q = cos * q_scale
        sin_q = sin_signed * q_scale

        def rope(t, c, s):
            return t * c + pltpu.roll(t, LANES // 2, axis=1) * s

        q = _dot(ub, wq_ref[...])
        k = _dot(ub, wk_ref[...])
        for hd in range(N_ATTN_HEADS):
            sl = slice(hd * LANES, (hd + 1) * LANES)
            q_ref[rows, sl] = rope(q[:, sl], cos_q, sin_q).astype(bf16)
            k_ref[rows, sl] = rope(k[:, sl], cos, sin_signed).astype(bf16)
        j, c0 = divmod(r0, ATTN_BLOCK)
        vt_ref[j, :, c0:c0 + half] = lax.dot_general(
            wvt_ref[...], u, _NT, preferred_element_type=f32).astype(bf16)
        z_ref[rows, :] = _dot(u, wz_ref[...]).astype(bf16)
        x_ref[rows, :] = _dot(u, wx_ref[...])
        dt_ref[rows, :] = _dot(u, wdt_ref[...])


def _in_proj(h, pos, invf, nw, wq, wk, wvt, wz, wx, wdt):
    t, d = h.shape
    tm = TOKEN_TILE
    kv_per_tile = tm // ATTN_BLOCK
    row = lambda i: (i, 0)

    def row_out(n, dtype):
        return pl.BlockSpec((tm, n), row), jax.ShapeDtypeStruct((t, n), dtype)

    outs = [row_out(wq.shape[1], bf16), row_out(wk.shape[1], bf16),
            (pl.BlockSpec((kv_per_tile, wvt.shape[0], ATTN_BLOCK), lambda i: (i, 0, 0)),
             jax.ShapeDtypeStruct((t // ATTN_BLOCK, wvt.shape[0], ATTN_BLOCK), bf16)),
            row_out(wz.shape[1], bf16), row_out(wx.shape[1], f32), row_out(wdt.shape[1], f32)]
    return pl.pallas_call(
        _in_proj_kernel,
        grid=(t // tm,),
        in_specs=[pl.BlockSpec((tm, d), row), pl.BlockSpec((tm, 1), row),
                  _const_spec(invf.shape), _const_spec(nw.shape), _const_spec(wq.shape),
                  _const_spec(wk.shape), _const_spec(wvt.shape), _const_spec(wz.shape),
                  _const_spec(wx.shape), _const_spec(wdt.shape)],
        out_specs=[spec for spec, _ in outs],
        out_shape=[shape for _, shape in outs],
        compiler_params=pltpu.CompilerParams(
            dimension_semantics=("arbitrary",), vmem_limit_bytes=VMEM_LIMIT_BYTES),
        name="in_proj",
    )(h, pos, invf, nw, wq, wk, wvt, wz, wx, wdt)


def _lam_kernel(lambda_init, q1_ref, k1_ref, q2_ref, k2_ref, o_ref):
    s1 = jnp.sum(q1_ref[...] * k1_ref[...], axis=-1, keepdims=True)
    s2 = jnp.sum(q2_ref[...] * k2_ref[...], axis=-1, keepdims=True)
    o_ref[...] = jnp.broadcast_to(jnp.exp(s1) - jnp.exp(s2) + lambda_init, o_ref.shape)


def _lam(lq1, lk1, lq2, lk2, lambda_init):
    return pl.pallas_call(
        functools.partial(_lam_kernel, lambda_init),
        out_shape=jax.ShapeDtypeStruct((1, LANES), f32),
        name="lam",
    )(lq1, lk1, lq2, lk2)


def _attn_kernel(out_scale, n_qblocks, lam_ref, q_ref, qn_ref, k_ref, vt_ref, sw_ref, o_ref,
                 qm_ref, qmn_ref, sa_ref, sb_ref, m_ref, l_ref, acc_ref):
    qi = pl.program_id(2)
    blk = ATTN_BLOCK
    half = blk // 2
    heads = range(ATTN_HEADS_PER_STEP)

    def lanes(hh):
        return slice(hh * LANES, (hh + 1) * LANES)

    def mask_q(hh, src_ref, dst_ref):
        q = src_ref[:, lanes(hh)]
        lane = lax.broadcasted_iota(jnp.int32, q.shape, 1)
        comp1 = (lane % ATTN_HEAD_DIM) < (ATTN_HEAD_DIM // 2)
        zero = jnp.zeros_like(q)
        dst_ref[hh, 0] = jnp.where(comp1, q, zero)
        dst_ref[hh, 1] = jnp.where(comp1, zero, q)

    def start(hh):
        mask_q(hh, q_ref, qm_ref)
        m_ref[hh] = jnp.full(m_ref.shape[1:], MASK_VALUE, f32)
        l_ref[hh] = jnp.zeros(l_ref.shape[1:], f32)
        acc_ref[hh] = jnp.zeros(acc_ref.shape[1:], f32)

    def scores(hh, j, dst_ref, queries_ref=qm_ref):
        kb = k_ref[j * blk:(j + 1) * blk, lanes(hh)]
        for c in range(2):
            dst_ref[hh, c] = lax.dot_general(kb, queries_ref[hh, c], _NT,
                                             preferred_element_type=f32)

    def scores_for_next_step(hh, dst_ref):
        mask_q(hh, qn_ref, qmn_ref)
        scores(hh, 0, dst_ref, qmn_ref)

    def update(hh, c, hq, s, vtb):
        qs = slice(hq * half, (hq + 1) * half)
        m_old = m_ref[hh, c, :, qs]
        m_new = jnp.maximum(m_old, jnp.max(s, axis=0, keepdims=True))
        alpha = jnp.exp2(m_old - m_new)
        p = jnp.exp2(s - m_new)
        l_ref[hh, c, :, qs] = alpha * l_ref[hh, c, :, qs] + jnp.sum(p, axis=0, keepdims=True)
        acc_ref[hh, c, :, qs] = alpha * acc_ref[hh, c, :, qs] + _dot(vtb, p.astype(bf16))
        m_ref[hh, c, :, qs] = m_new

    def full_block(hh, j, src_ref):
        vtb = vt_ref[j, lanes(hh), :]
        for c in range(2):
            for hq in range(2):
                update(hh, c, hq, src_ref[hh, c, :, hq * half:(hq + 1) * half], vtb)

    def diagonal_block(hh, j, src_ref):
        vtb = vt_ref[j, lanes(hh), :]
        key = lax.broadcasted_iota(jnp.int32, (half, half), 0)
        qry = lax.broadcasted_iota(jnp.int32, (half, half), 1)
        tri = key <= qry
        for c in range(2):
            update(hh, c, 0, jnp.where(tri, src_ref[hh, c, 0:half, 0:half], MASK_VALUE),
                   vtb[:, 0:half])
            s_lo = jnp.where(tri, src_ref[hh, c, half:blk, half:blk], MASK_VALUE)
            update(hh, c, 1, jnp.concatenate([src_ref[hh, c, 0:half, half:blk], s_lo], axis=0), vtb)

    def finish(hh):
        lam = lam_ref[:, 0:1]
        o_t = (acc_ref[hh, 0] * (1.0 / l_ref[hh, 0])
               - lam * (acc_ref[hh, 1] * (1.0 / l_ref[hh, 1])))
        o_ref[:, lanes(hh)] = (_rms(o_t.T, sw_ref[...]) * out_scale).astype(bf16)

    bufs = (sa_ref, sb_ref)
    for qb in range(n_qblocks):
        first = (qb * (qb + 1) // 2) % 2

        @pl.when(qi == qb)
        def _(qb=qb, first=first):
            for hh in heads:
                start(hh)
            if qb == 0:
                for hh in heads:
                    scores(hh, 0, bufs[first])
            for j in range(qb):
                for hh in heads:
                    scores(hh, j + 1, bufs[(first + j + 1) % 2])
                for hh in heads:
                    full_block(hh, j, bufs[(first + j) % 2])
            if qb + 1 < n_qblocks:
                for hh in heads:
                    scores_for_next_step(hh, bufs[(first + qb + 1) % 2])
            for hh in heads:
                diagonal_block(hh, qb, bufs[(first + qb) % 2])
            for hh in heads:
                finish(hh)


def _attention(lam, q, k, vt, subln, batch, seq, out_scale):
    t = q.shape[0]
    blk = ATTN_BLOCK
    nq = seq // blk
    hps = ATTN_HEADS_PER_STEP
    width = hps * LANES
    return pl.pallas_call(
        functools.partial(_attn_kernel, out_scale, nq),
        grid=(batch, N_ATTN_HEADS // hps, nq),
        in_specs=[pl.BlockSpec((1, LANES), lambda b, h, i: (0, 0)),
                  pl.BlockSpec((blk, width), lambda b, h, i: (b * nq + i, h)),
                  pl.BlockSpec((blk, width), lambda b, h, i: (b * nq + jnp.minimum(i + 1, nq - 1), h)),
                  pl.BlockSpec((seq, width), lambda b, h, i: (b, h)),
                  pl.BlockSpec((nq, hps * ATTN_V_DIM, blk), lambda b, h, i: (b, h, 0)),
                  pl.BlockSpec((1, ATTN_V_DIM), lambda b, h, i: (0, 0))],
        out_specs=pl.BlockSpec((blk, hps * ATTN_V_DIM), lambda b, h, i: (b * nq + i, h)),
        out_shape=jax.ShapeDtypeStruct((t, N_ATTN_HEADS * ATTN_V_DIM), bf16),
        scratch_shapes=[pltpu.VMEM((hps, 2, blk, LANES), bf16), pltpu.VMEM((hps, 2, blk, LANES), bf16),
                        pltpu.VMEM((hps, 2, blk, blk), f32), pltpu.VMEM((hps, 2, blk, blk), f32),
                        pltpu.VMEM((hps, 2, 1, blk), f32), pltpu.VMEM((hps, 2, 1, blk), f32),
                        pltpu.VMEM((hps, 2, ATTN_V_DIM, blk), f32)],
        compiler_params=pltpu.CompilerParams(
            dimension_semantics=("arbitrary", "arbitrary", "arbitrary")),
        name="diff_attn",
    )(lam, q, q, k, vt, subln)


def _split_bf16(x, parts):
    out = []
    for _ in range(parts - 1):
        hi = x.astype(bf16)
        out.append(hi)
        x = x - hi.astype(f32)
    out.append(x.astype(bf16))
    return out


def _ssd_kernel(xbc_ref, dt_ref, z_ref, cw_ref, cb_ref, dtb_ref, alog_ref, dskip_ref, nw_ref,
                o_ref, tail_ref, state_ref):
    step = pl.program_id(1)
    L = SSD_CHUNK
    width = SSD_HEADS * SSD_HEAD_DIM
    bc_width = SSD_GROUPS * SSD_STATE

    @pl.when(step == 0)
    def _():
        tail_ref[...] = jnp.zeros(tail_ref.shape, f32)
        state_ref[...] = jnp.zeros(state_ref.shape, f32)

    row = lax.broadcasted_iota(jnp.int32, (L, L), 0)
    col = lax.broadcasted_iota(jnp.int32, (L, L), 1)
    tril = row >= col
    tril_b = tril.astype(bf16)
    hrow = lax.broadcasted_iota(jnp.int32, (LANES, width), 0)
    hcol = lax.broadcasted_iota(jnp.int32, (LANES, width), 1)
    expand = (hcol // SSD_HEAD_DIM == hrow).astype(bf16)
    gcol = lax.broadcasted_iota(jnp.int32, (L, SSD_GROUP_WIDTH), 1) // SSD_HEAD_DIM
    zero_g = jnp.zeros((L, SSD_GROUP_WIDTH), bf16)
    neg_a = -jnp.exp(alog_ref[...]) * math.log2(math.e)
    w0, w1, w2, w3 = (cw_ref[k:k + 1, :] for k in range(SSD_CONV))

    for i in range(SSD_CHUNKS_PER_STEP):
        rows = slice(i * L, (i + 1) * L)
        if i == 0:
            ext = jnp.concatenate([tail_ref[...], xbc_ref[rows, :]], axis=0)
        else:
            ext = xbc_ref[i * L - SUBLANES:(i + 1) * L, :]
        ext1 = pltpu.roll(ext, 1, axis=0)
        pair_lo = w1 * ext + w0 * ext1
        pair_hi = w3 * ext[SUBLANES:, :] + w2 * ext1[SUBLANES:, :]
        conv = cb_ref[...] + pair_hi + pltpu.roll(pair_lo, 2, axis=0)[SUBLANES:, :]
        act = _silu(conv)
        xs = act[:, :width]
        bm = act[:, width:width + bc_width]
        cm = act[:, width + bc_width:]

        dt = jax.nn.softplus(dt_ref[rows, :] + dtb_ref[...])
        d_a = dt * neg_a
        a_cum = sum(_dot(tril_b, part) for part in _split_bf16(d_a, 3))
        a_cum_t = a_cum.T
        a_end = a_cum[L - 1:L, :]
        decay_to_end = jnp.exp2(a_end - a_cum)
        exp_a = jnp.exp2(a_cum)

        dt_x = _dot(dt.astype(bf16), expand)
        dte_x = _dot(decay_to_end.astype(bf16), expand)
        ea_hi, ea_lo = _split_bf16(exp_a, 2)
        ea_x = _dot(ea_hi, expand) + _dot(ea_lo, expand)

        x_dt = xs * dt_x
        x_dt_b = x_dt.astype(bf16)
        x_end_b = (x_dt * dte_x).astype(bf16)

        for g in range(SSD_GROUPS):
            gs = slice(g * SSD_GROUP_WIDTH, (g + 1) * SSD_GROUP_WIDTH)
            ns = slice(g * SSD_STATE, (g + 1) * SSD_STATE)
            b_g = bm[:, ns]
            c_g = cm[:, ns].astype(bf16)
            cb = lax.dot_general(c_g, b_g.astype(bf16), _NT, preferred_element_type=f32)
            b_t = b_g.T.astype(bf16)
            st = state_ref[g]
            y_g = _dot(c_g, st.astype(bf16)) * ea_x[:, gs]
            state_ref[g] = st * ea_x[L - 1:L, gs] + _dot(b_t, x_end_b[:, gs])
            x_g = x_dt_b[:, gs]
            m_heads = []
            for r in range(SSD_HEADS_PER_GROUP):
                h = g * SSD_HEADS_PER_GROUP + r
                seg = a_cum[:, h:h + 1] - a_cum_t[h:h + 1, :]
                decay = jnp.exp2(jnp.where(tril, seg, MASK_VALUE))
                m_heads.append((cb * decay).astype(bf16))
            x_bd = jnp.concatenate([jnp.where(gcol == r, x_g, zero_g)
                                    for r in range(SSD_HEADS_PER_GROUP)], axis=0)
            y_g = y_g + _dot(jnp.concatenate(m_heads, axis=1), x_bd) + dskip_ref[:, gs] * xs[:, gs]
            y_g = y_g * _silu(z_ref[rows, gs].astype(f32))
            o_ref[rows, gs] = _rms(y_g, nw_ref[:, gs]).astype(bf16)

    tail_ref[...] = xbc_ref[SSD_CHUNKS_PER_STEP * L - SUBLANES:SSD_CHUNKS_PER_STEP * L, :]


def _ssd(xbc, dt_raw, z, conv_w, conv_b, dt_bias, a_log, d_skip, ssd_norm, batch, seq):
    t, ch = xbc.shape
    rows = SSD_CHUNKS_PER_STEP * SSD_CHUNK
    steps = seq // rows
    width = SSD_HEADS * SSD_HEAD_DIM
    row = lambda b, s: (b * steps + s, 0)
    return pl.pallas_call(
        _ssd_kernel,
        grid=(batch, steps),
        in_specs=[pl.BlockSpec((rows, ch), row), pl.BlockSpec((rows, LANES), row),
                  pl.BlockSpec((rows, width), row),
                  _const_spec(conv_w.shape), _const_spec(conv_b.shape), _const_spec(dt_bias.shape),
                  _const_spec(a_log.shape), _const_spec(d_skip.shape), _const_spec(ssd_norm.shape)],
        out_specs=pl.BlockSpec((rows, width), row),
        out_shape=jax.ShapeDtypeStruct((t, width), bf16),
        scratch_shapes=[pltpu.VMEM((SUBLANES, ch), f32),
                        pltpu.VMEM((SSD_GROUPS, SSD_STATE, SSD_GROUP_WIDTH), f32)],
        compiler_params=pltpu.CompilerParams(dimension_semantics=("arbitrary", "arbitrary")),
        name="ssd",
    )(xbc, dt_raw, z, conv_w, conv_b, dt_bias, a_log, d_skip, ssd_norm)


def _out_ffn2_kernel(final, h_ref, o_ref, y_ref, woo_ref, woy_ref, nw_ref, wg_ref, wu_ref, wd_ref,
                     fw_ref, out_ref):
    half = h_ref.shape[0] // 2
    for r0 in (0, half):
        rows = slice(r0, r0 + half)
        h2 = h_ref[rows, :] + _dot(o_ref[rows, :], woo_ref[...]) + _dot(y_ref[rows, :], woy_ref[...])
        h3 = h2 + 0.5 * _swiglu_of_rms(h2, nw_ref, wg_ref, wu_ref, wd_ref)
        out_ref[rows, :] = _rms(h3, fw_ref[...]) if final else h3


def _out_ffn2(h, o, y, woo, woy, nw, wg, wu, wd, fw, final):
    t, d = h.shape
    tm = TOKEN_TILE
    row = lambda i: (i, 0)
    return pl.pallas_call(
        functools.partial(_out_ffn2_kernel, final),
        grid=(t // tm,),
        in_specs=[pl.BlockSpec((tm, d), row), pl.BlockSpec((tm, o.shape[1]), row),
                  pl.BlockSpec((tm, y.shape[1]), row),
                  _const_spec(woo.shape), _const_spec(woy.shape), _const_spec(nw.shape),
                  _const_spec(wg.shape), _const_spec(wu.shape), _const_spec(wd.shape),
                  _const_spec(fw.shape)],
        out_specs=pl.BlockSpec((tm, d), row),
        out_shape=jax.ShapeDtypeStruct((t, d), f32),
        compiler_params=pltpu.CompilerParams(
            dimension_semantics=("arbitrary",), vmem_limit_bytes=VMEM_LIMIT_BYTES),
        name="out_ffn2",
    )(h, o, y, woo, woy, nw, wg, wu, wd, fw)


def kernel(x, positions, ffn1_norm, ffn1_w_gate, ffn1_w_up, ffn1_w_down, mix_norm, w_in, lambda_q1, lambda_k1, lambda_q2, lambda_k2, attn_subln, conv_w, conv_b, dt_bias, a_log, d_skip, ssd_norm, w_out, ffn2_norm, ffn2_w_gate, ffn2_w_up, ffn2_w_down, final_norm):
    batch, seq, d = x.shape
    depth = ffn1_norm.shape[0]
    t = batch * seq
    qk_cols = N_ATTN_HEADS * 2 * ATTN_HEAD_DIM
    attn_width = N_ATTN_HEADS * ATTN_V_DIM
    ssd_width = SSD_HEADS * SSD_HEAD_DIM
    conv_ch = ssd_width + 2 * SSD_GROUPS * SSD_STATE
    splits = [0, qk_cols, 2 * qk_cols, 2 * qk_cols + attn_width, 2 * qk_cols + attn_width + ssd_width,
              2 * qk_cols + attn_width + ssd_width + conv_ch]

    inv = 1.0 / (ROPE_THETA ** (jnp.arange(0, ATTN_HEAD_DIM, 2, dtype=f32) / ATTN_HEAD_DIM))
    invf = jnp.tile(inv, LANES // inv.shape[0])[None, :]
    pos = positions.reshape(t, 1)
    half_dim = ATTN_HEAD_DIM // 2
    qk_order = jnp.arange(qk_cols).reshape(N_ATTN_HEADS, 2, 2, half_dim).transpose(0, 2, 1, 3).reshape(-1)
    pad_heads = lambda a: jnp.pad(a.astype(f32), (0, LANES - SSD_HEADS))[None, :]
    row2d = lambda a: a.astype(f32)[None, :]

    h = x.reshape(t, d)
    for layer in range(depth):
        lambda_init = 0.8 - 0.6 * math.exp(-0.3 * layer)
        last = layer == depth - 1
        h1 = _ffn1(h, row2d(ffn1_norm[layer]), ffn1_w_gate[layer].astype(bf16),
                   ffn1_w_up[layer].astype(bf16), ffn1_w_down[layer].astype(bf16))

        w = w_in[layer]
        wq, wk, wv, wz, wx = (w[:, splits[i]:splits[i + 1]].astype(bf16) for i in range(5))
        wdt = jnp.pad(w[:, splits[5]:], ((0, 0), (0, LANES - SSD_HEADS))).astype(bf16)
        q, k, vt, z, xbc, dt_raw = _in_proj(h1, pos, invf, row2d(mix_norm[layer]),
                                            wq[:, qk_order], wk[:, qk_order], wv.T, wz, wx, wdt)

        lam = _lam(row2d(lambda_q1[layer]), row2d(lambda_k1[layer]), row2d(lambda_q2[layer]),
                   row2d(lambda_k2[layer]), lambda_init)
        o = _attention(lam, q, k, vt, row2d(attn_subln[layer]), batch, seq, 1.0 - lambda_init)

        y = _ssd(xbc, dt_raw, z, conv_w[layer].astype(f32), row2d(conv_b[layer]),
                 pad_heads(dt_bias[layer]), pad_heads(a_log[layer]),
                 row2d(jnp.repeat(d_skip[layer], SSD_HEAD_DIM)), row2d(ssd_norm[layer]), batch, seq)

        wo = w_out[layer].astype(bf16)
        h = _out_ffn2(h1, o, y, wo[:attn_width], wo[attn_width:], row2d(ffn2_norm[layer]),
                      ffn2_w_gate[layer].astype(bf16), ffn2_w_up[layer].astype(bf16),
                      ffn2_w_down[layer].astype(bf16), row2d(final_norm), final=last)
    return h.reshape(batch, seq, d)
```

```python
import functools
import math

import jax
import jax.numpy as jnp
from jax import lax
from jax.experimental import pallas as pl
from jax.experimental.pallas import tpu as pltpu

NORM_EPS = 1e-6
ROPE_THETA = 10000.0
N_ATTN_HEADS = 8
ATTN_HEAD_DIM = 64
ATTN_V_DIM = 2 * ATTN_HEAD_DIM
SSD_HEADS = 16
SSD_HEAD_DIM = 64
SSD_GROUPS = 4
SSD_HEADS_PER_GROUP = SSD_HEADS // SSD_GROUPS
SSD_STATE = 128
SSD_CONV = 4
SSD_CHUNK = 128
SSD_CHUNKS_PER_STEP = 8
SSD_GROUP_WIDTH = SSD_HEADS_PER_GROUP * SSD_HEAD_DIM

LANES = 128
SUBLANES = 8
VMEM_LIMIT_BYTES = 56 * 1024 * 1024

TOKEN_TILE = 512
ATTN_BLOCK = 512
ATTN_HEADS_PER_STEP = 1
MASK_VALUE = -1e30

_NT = (((1,), (1,)), ((), ()))

bf16 = jnp.bfloat16
f32 = jnp.float32


def _inv_rms(x):
    return lax.rsqrt(jnp.mean(x * x, axis=-1, keepdims=True) + NORM_EPS)


def _rms(x, w):
    return x * _inv_rms(x) * w


def _silu(x):
    return x * jax.nn.sigmoid(x)


def _dot(a, b):
    return jnp.dot(a, b, preferred_element_type=f32)


def _const_spec(shape):
    zeros = (0,) * len(shape)
    return pl.BlockSpec(shape, lambda *_: zeros, pipeline_mode=pl.Buffered(1))


def _swiglu_of_rms(x, nw_ref, wg_ref, wu_ref, wd_ref):
    xw = (x * nw_ref[...]).astype(bf16)
    r = _inv_rms(x)
    g = _dot(xw, wg_ref[...]) * r
    u = _dot(xw, wu_ref[...]) * r
    a = (_silu(g) * u).astype(bf16)
    return _dot(a, wd_ref[...])


def _ffn1_kernel(x_ref, nw_ref, wg_ref, wu_ref, wd_ref, o_ref):
    half = x_ref.shape[0] // 2
    for r0 in (0, half):
        x = x_ref[r0:r0 + half, :]
        o_ref[r0:r0 + half, :] = x + 0.5 * _swiglu_of_rms(x, nw_ref, wg_ref, wu_ref, wd_ref)


def _ffn1(x, nw, wg, wu, wd):
    t, d = x.shape
    tm = TOKEN_TILE
    return pl.pallas_call(
        _ffn1_kernel,
        grid=(t // tm,),
        in_specs=[pl.BlockSpec((tm, d), lambda i: (i, 0)),
                  _const_spec(nw.shape), _const_spec(wg.shape), _const_spec(wu.shape),
                  _const_spec(wd.shape)],
        out_specs=pl.BlockSpec((tm, d), lambda i: (i, 0)),
        out_shape=jax.ShapeDtypeStruct((t, d), f32),
        compiler_params=pltpu.CompilerParams(
            dimension_semantics=("arbitrary",), vmem_limit_bytes=VMEM_LIMIT_BYTES),
        name="ffn1",
    )(x, nw, wg, wu, wd)


def _in_proj_kernel(h_ref, pos_ref, invf_ref, nw_ref, wq_ref, wk_ref, wvt_ref, wz_ref, wx_ref,
                    wdt_ref, q_ref, k_ref, vt_ref, z_ref, x_ref, dt_ref):
    half = h_ref.shape[0] // 2
    assert ATTN_BLOCK % half == 0
    for r0 in (0, half):
        rows = slice(r0, r0 + half)
        h = h_ref[rows, :]
        hw = h * nw_ref[...]
        r = _inv_rms(h)
        ub = hw.astype(bf16)
        u = (hw * r).astype(bf16)

        ang = pos_ref[rows, :].astype(f32) * invf_ref[...]
        cos = jnp.cos(ang) * r
        sin = jnp.sin(ang) * r
        lane = lax.broadcasted_iota(jnp.int32, ang.shape, 1)
        sin_signed = jnp.where(lane < LANES // 2, -sin, sin)
        q_scale = ATTN_HEAD_DIM ** -0.5 * math.log2(math.e)
        cos_q = cos * q_scale
        sin_q = sin_signed * q_scale

        def rope(t, c, s):
            return t * c + pltpu.roll(t, LANES // 2, axis=1) * s

        q = _dot(ub, wq_ref[...])
        k = _dot(ub, wk_ref[...])
        for hd in range(N_ATTN_HEADS):
            sl = slice(hd * LANES, (hd + 1) * LANES)
            q_ref[rows, sl] = rope(q[:, sl], cos_q, sin_q).astype(bf16)
            k_ref[rows, sl] = rope(k[:, sl], cos, sin_signed).astype(bf16)
        j, c0 = divmod(r0, ATTN_BLOCK)
        vt_ref[j, :, c0:c0 + half] = lax.dot_general(
            wvt_ref[...], u, _NT, preferred_element_type=f32).astype(bf16)
        z_ref[rows, :] = _dot(u, wz_ref[...]).astype(bf16)
        x_ref[rows, :] = _dot(u, wx_ref[...])
        dt_ref[rows, :] = _dot(u, wdt_ref[...])


def _in_proj(h, pos, invf, nw, wq, wk, wvt, wz, wx, wdt):
    t, d = h.shape
    tm = TOKEN_TILE
    kv_per_tile = tm // ATTN_BLOCK
    row = lambda i: (i, 0)

    def row_out(n, dtype):
        return pl.BlockSpec((tm, n), row), jax.ShapeDtypeStruct((t, n), dtype)

    outs = [row_out(wq.shape[1], bf16), row_out(wk.shape[1], bf16),
            (pl.BlockSpec((kv_per_tile, wvt.shape[0], ATTN_BLOCK), lambda i: (i, 0, 0)),
             jax.ShapeDtypeStruct((t // ATTN_BLOCK, wvt.shape[0], ATTN_BLOCK), bf16)),
            row_out(wz.shape[1], bf16), row_out(wx.shape[1], f32), row_out(wdt.shape[1], f32)]
    return pl.pallas_call(
        _in_proj_kernel,
        grid=(t // tm,),
        in_specs=[pl.BlockSpec((tm, d), row), pl.BlockSpec((tm, 1), row),
                  _const_spec(invf.shape), _const_spec(nw.shape), _const_spec(wq.shape),
                  _const_spec(wk.shape), _const_spec(wvt.shape), _const_spec(wz.shape),
                  _const_spec(wx.shape), _const_spec(wdt.shape)],
        out_specs=[spec for spec, _ in outs],
        out_shape=[shape for _, shape in outs],
        compiler_params=pltpu.CompilerParams(
            dimension_semantics=("arbitrary",), vmem_limit_bytes=VMEM_LIMIT_BYTES),
        name="in_proj",
    )(h, pos, invf, nw, wq, wk, wvt, wz, wx, wdt)


def _lam_kernel(lambda_init, q1_ref, k1_ref, q2_ref, k2_ref, o_ref):
    s1 = jnp.sum(q1_ref[...] * k1_ref[...], axis=-1, keepdims=True)
    s2 = jnp.sum(q2_ref[...] * k2_ref[...], axis=-1, keepdims=True)
    o_ref[...] = jnp.broadcast_to(jnp.exp(s1) - jnp.exp(s2) + lambda_init, o_ref.shape)


def _lam(lq1, lk1, lq2, lk2, lambda_init):
    return pl.pallas_call(
        functools.partial(_lam_kernel, lambda_init),
        out_shape=jax.ShapeDtypeStruct((1, LANES), f32),
        name="lam",
    )(lq1, lk1, lq2, lk2)


def _attn_kernel(out_scale, n_qblocks, lam_ref, q_ref, qn_ref, k_ref, vt_ref, sw_ref, o_ref,
                 qm_ref, qmn_ref, sa_ref, sb_ref, m_ref, l_ref, acc_ref):
    qi = pl.program_id(2)
    blk = ATTN_BLOCK
    half = blk // 2
    heads = range(ATTN_HEADS_PER_STEP)

    def lanes(hh):
        return slice(hh * LANES, (hh + 1) * LANES)

    def mask_q(hh, src_ref, dst_ref):
        q = src_ref[:, lanes(hh)]
        lane = lax.broadcasted_iota(jnp.int32, q.shape, 1)
        comp1 = (lane % ATTN_HEAD_DIM) < (ATTN_HEAD_DIM // 2)
        zero = jnp.zeros_like(q)
        dst_ref[hh, 0] = jnp.where(comp1, q, zero)
        dst_ref[hh, 1] = jnp.where(comp1, zero, q)

    def start(hh):
        mask_q(hh, q_ref, qm_ref)
        m_ref[hh] = jnp.full(m_ref.shape[1:], MASK_VALUE, f32)
        l_ref[hh] = jnp.zeros(l_ref.shape[1:], f32)
        acc_ref[hh] = jnp.zeros(acc_ref.shape[1:], f32)

    def scores(hh, j, dst_ref, queries_ref=qm_ref, diagonal=False):
        kb = k_ref[j * blk:(j + 1) * blk, lanes(hh)]
        for c in range(2):
            if diagonal:
                dst_ref[hh, c, 0:half, :] = lax.dot_general(
                    kb[0:half], queries_ref[hh, c], _NT, preferred_element_type=f32)
                dst_ref[hh, c, half:blk, half:blk] = lax.dot_general(
                    kb[half:blk], queries_ref[hh, c, half:blk, :], _NT, preferred_element_type=f32)
            else:
                dst_ref[hh, c] = lax.dot_general(kb, queries_ref[hh, c], _NT,
                                                 preferred_element_type=f32)

    def scores_for_next_step(hh, dst_ref):
        mask_q(hh, qn_ref, qmn_ref)
        scores(hh, 0, dst_ref, qmn_ref)

    def update(hh, c, hq, s, vtb):
        qs = slice(hq * half, (hq + 1) * half)
        m_old = m_ref[hh, c, :, qs]
        m_new = jnp.maximum(m_old, jnp.max(s, axis=0, keepdims=True))
        alpha = jnp.exp2(m_old - m_new)
        p = jnp.exp2(s - m_new)
        l_ref[hh, c, :, qs] = alpha * l_ref[hh, c, :, qs] + jnp.sum(p, axis=0, keepdims=True)
        acc_ref[hh, c, :, qs] = alpha * acc_ref[hh, c, :, qs] + _dot(vtb, p.astype(bf16))
        m_ref[hh, c, :, qs] = m_new

    def full_block(hh, j, src_ref):
        vtb = vt_ref[j, lanes(hh), :]
        for c in range(2):
            for hq in range(2):
                update(hh, c, hq, src_ref[hh, c, :, hq * half:(hq + 1) * half], vtb)

    def diagonal_block(hh, j, src_ref):
        vtb = vt_ref[j, lanes(hh), :]
        key = lax.broadcasted_iota(jnp.int32, (half, half), 0)
        qry = lax.broadcasted_iota(jnp.int32, (half, half), 1)
        tri = key <= qry
        for c in range(2):
            update(hh, c, 0, jnp.where(tri, src_ref[hh, c, 0:half, 0:half], MASK_VALUE),
                   vtb[:, 0:half])
            s_lo = jnp.where(tri, src_ref[hh, c, half:blk, half:blk], MASK_VALUE)
            update(hh, c, 1, jnp.concatenate([src_ref[hh, c, 0:half, half:blk], s_lo], axis=0), vtb)

    def finish(hh):
        lam = lam_ref[:, 0:1]
        o_t = (acc_ref[hh, 0] * (1.0 / l_ref[hh, 0])
               - lam * (acc_ref[hh, 1] * (1.0 / l_ref[hh, 1])))
        o_ref[:, lanes(hh)] = (_rms(o_t.T, sw_ref[...]) * out_scale).astype(bf16)

    bufs = (sa_ref, sb_ref)
    for qb in range(n_qblocks):
        first = (qb * (qb + 1) // 2) % 2

        @pl.when(qi == qb)
        def _(qb=qb, first=first):
            for hh in heads:
                start(hh)
            if qb == 0:
                for hh in heads:
                    scores(hh, 0, bufs[first], diagonal=True)
            for j in range(qb):
                for hh in heads:
                    scores(hh, j + 1, bufs[(first + j + 1) % 2], diagonal=(j + 1 == qb))
                for hh in heads:
                    full_block(hh, j, bufs[(first + j) % 2])
            if qb + 1 < n_qblocks:
                for hh in heads:
                    scores_for_next_step(hh, bufs[(first + qb + 1) % 2])
            for hh in heads:
                diagonal_block(hh, qb, bufs[(first + qb) % 2])
            for hh in heads:
                finish(hh)


def _attention(lam, q, k, vt, subln, batch, seq, out_scale):
    t = q.shape[0]
    blk = ATTN_BLOCK
    nq = seq // blk
    hps = ATTN_HEADS_PER_STEP
    width = hps * LANES
    return pl.pallas_call(
        functools.partial(_attn_kernel, out_scale, nq),
        grid=(batch, N_ATTN_HEADS // hps, nq),
        in_specs=[pl.BlockSpec((1, LANES), lambda b, h, i: (0, 0)),
                  pl.BlockSpec((blk, width), lambda b, h, i: (b * nq + i, h)),
                  pl.BlockSpec((blk, width), lambda b, h, i: (b * nq + jnp.minimum(i + 1, nq - 1), h)),
                  pl.BlockSpec((seq, width), lambda b, h, i: (b, h)),
                  pl.BlockSpec((nq, hps * ATTN_V_DIM, blk), lambda b, h, i: (b, h, 0)),
                  pl.BlockSpec((1, ATTN_V_DIM), lambda b, h, i: (0, 0))],
        out_specs=pl.BlockSpec((blk, hps * ATTN_V_DIM), lambda b, h, i: (b * nq + i, h)),
        out_shape=jax.ShapeDtypeStruct((t, N_ATTN_HEADS * ATTN_V_DIM), bf16),
        scratch_shapes=[pltpu.VMEM((hps, 2, blk, LANES), bf16), pltpu.VMEM((hps, 2, blk, LANES), bf16),
                        pltpu.VMEM((hps, 2, blk, blk), f32), pltpu.VMEM((hps, 2, blk, blk), f32),
                        pltpu.VMEM((hps, 2, 1, blk), f32), pltpu.VMEM((hps, 2, 1, blk), f32),
                        pltpu.VMEM((hps, 2, ATTN_V_DIM, blk), f32)],
        compiler_params=pltpu.CompilerParams(
            dimension_semantics=("arbitrary", "arbitrary", "arbitrary")),
        name="diff_attn",
    )(lam, q, q, k, vt, subln)


def _split_bf16(x, parts):
    out = []
    for _ in range(parts - 1):
        hi = x.astype(bf16)
        out.append(hi)
        x = x - hi.astype(f32)
    out.append(x.astype(bf16))
    return out


def _ssd_kernel(xbc_ref, dt_ref, z_ref, cw_ref, cb_ref, dtb_ref, alog_ref, dskip_ref, nw_ref,
                o_ref, tail_ref, state_ref):
    step = pl.program_id(1)
    L = SSD_CHUNK
    width = SSD_HEADS * SSD_HEAD_DIM
    bc_width = SSD_GROUPS * SSD_STATE

    @pl.when(step == 0)
    def _():
        tail_ref[...] = jnp.zeros(tail_ref.shape, f32)
        state_ref[...] = jnp.zeros(state_ref.shape, f32)

    row = lax.broadcasted_iota(jnp.int32, (L, L), 0)
    col = lax.broadcasted_iota(jnp.int32, (L, L), 1)
    tril = row >= col
    tril_b = tril.astype(bf16)
    hrow = lax.broadcasted_iota(jnp.int32, (LANES, width), 0)
    hcol = lax.broadcasted_iota(jnp.int32, (LANES, width), 1)
    expand = (hcol // SSD_HEAD_DIM == hrow).astype(bf16)
    gcol = lax.broadcasted_iota(jnp.int32, (L, SSD_GROUP_WIDTH), 1) // SSD_HEAD_DIM
    zero_g = jnp.zeros((L, SSD_GROUP_WIDTH), bf16)
    neg_a = -jnp.exp(alog_ref[...]) * math.log2(math.e)
    w0, w1, w2, w3 = (cw_ref[k:k + 1, :] for k in range(SSD_CONV))

    for i in range(SSD_CHUNKS_PER_STEP):
        rows = slice(i * L, (i + 1) * L)
        if i == 0:
            ext = jnp.concatenate([tail_ref[...], xbc_ref[rows, :]], axis=0)
        else:
            ext = xbc_ref[i * L - SUBLANES:(i + 1) * L, :]
        ext1 = pltpu.roll(ext, 1, axis=0)
        pair_lo = w1 * ext + w0 * ext1
        pair_hi = w3 * ext[SUBLANES:, :] + w2 * ext1[SUBLANES:, :]
        conv = cb_ref[...] + pair_hi + pltpu.roll(pair_lo, 2, axis=0)[SUBLANES:, :]
        act = _silu(conv)
        xs = act[:, :width]
        bm = act[:, width:width + bc_width]
        cm = act[:, width + bc_width:]

        dt = jax.nn.softplus(dt_ref[rows, :] + dtb_ref[...])
        d_a = dt * neg_a
        a_cum = sum(_dot(tril_b, part) for part in _split_bf16(d_a, 3))
        a_cum_t = a_cum.T
        a_end = a_cum[L - 1:L, :]
        decay_to_end = jnp.exp2(a_end - a_cum)
        exp_a = jnp.exp2(a_cum)

        dt_x = _dot(dt.astype(bf16), expand)
        dte_x = _dot(decay_to_end.astype(bf16), expand)
        ea_hi, ea_lo = _split_bf16(exp_a, 2)
        ea_x = _dot(ea_hi, expand) + _dot(ea_lo, expand)

        x_dt = xs * dt_x
        x_dt_b = x_dt.astype(bf16)
        x_end_b = (x_dt * dte_x).astype(bf16)

        for g in range(SSD_GROUPS):
            gs = slice(g * SSD_GROUP_WIDTH, (g + 1) * SSD_GROUP_WIDTH)
            ns = slice(g * SSD_STATE, (g + 1) * SSD_STATE)
            b_g = bm[:, ns]
            c_g = cm[:, ns].astype(bf16)
            cb = lax.dot_general(c_g, b_g.astype(bf16), _NT, preferred_element_type=f32)
            b_t = b_g.T.astype(bf16)
            st = state_ref[g]
            y_g = _dot(c_g, st.astype(bf16)) * ea_x[:, gs]
            state_ref[g] = st * ea_x[L - 1:L, gs] + _dot(b_t, x_end_b[:, gs])
            x_g = x_dt_b[:, gs]
            m_heads = []
            for r in range(SSD_HEADS_PER_GROUP):
                h = g * SSD_HEADS_PER_GROUP + r
                seg = a_cum[:, h:h + 1] - a_cum_t[h:h + 1, :]
                decay = jnp.exp2(jnp.where(tril, seg, MASK_VALUE))
                m_heads.append((cb * decay).astype(bf16))
            x_bd = jnp.concatenate([jnp.where(gcol == r, x_g, zero_g)
                                    for r in range(SSD_HEADS_PER_GROUP)], axis=0)
            y_g = y_g + _dot(jnp.concatenate(m_heads, axis=1), x_bd) + dskip_ref[:, gs] * xs[:, gs]
            y_g = y_g * _silu(z_ref[rows, gs].astype(f32))
            o_ref[rows, gs] = _rms(y_g, nw_ref[:, gs]).astype(bf16)

    tail_ref[...] = xbc_ref[SSD_CHUNKS_PER_STEP * L - SUBLANES:SSD_CHUNKS_PER_STEP * L, :]


def _ssd(xbc, dt_raw, z, conv_w, conv_b, dt_bias, a_log, d_skip, ssd_norm, batch, seq):
    t, ch = xbc.shape
    rows = SSD_CHUNKS_PER_STEP * SSD_CHUNK
    steps = seq // rows
    width = SSD_HEADS * SSD_HEAD_DIM
    row = lambda b, s: (b * steps + s, 0)
    return pl.pallas_call(
        _ssd_kernel,
        grid=(batch, steps),
        in_specs=[pl.BlockSpec((rows, ch), row), pl.BlockSpec((rows, LANES), row),
                  pl.BlockSpec((rows, width), row),
                  _const_spec(conv_w.shape), _const_spec(conv_b.shape), _const_spec(dt_bias.shape),
                  _const_spec(a_log.shape), _const_spec(d_skip.shape), _const_spec(ssd_norm.shape)],
        out_specs=pl.BlockSpec((rows, width), row),
        out_shape=jax.ShapeDtypeStruct((t, width), bf16),
        scratch_shapes=[pltpu.VMEM((SUBLANES, ch), f32),
                        pltpu.VMEM((SSD_GROUPS, SSD_STATE, SSD_GROUP_WIDTH), f32)],
        compiler_params=pltpu.CompilerParams(dimension_semantics=("arbitrary", "arbitrary")),
        name="ssd",
    )(xbc, dt_raw, z, conv_w, conv_b, dt_bias, a_log, d_skip, ssd_norm)


def _out_ffn2_kernel(final, h_ref, o_ref, y_ref, woo_ref, woy_ref, nw_ref, wg_ref, wu_ref, wd_ref,
                     fw_ref, out_ref):
    half = h_ref.shape[0] // 2
    for r0 in (0, half):
        rows = slice(r0, r0 + half)
        h2 = h_ref[rows, :] + _dot(o_ref[rows, :], woo_ref[...]) + _dot(y_ref[rows, :], woy_ref[...])
        h3 = h2 + 0.5 * _swiglu_of_rms(h2, nw_ref, wg_ref, wu_ref, wd_ref)
        out_ref[rows, :] = _rms(h3, fw_ref[...]) if final else h3


def _out_ffn2(h, o, y, woo, woy, nw, wg, wu, wd, fw, final):
    t, d = h.shape
    tm = TOKEN_TILE
    row = lambda i: (i, 0)
    return pl.pallas_call(
        functools.partial(_out_ffn2_kernel, final),
        grid=(t // tm,),
        in_specs=[pl.BlockSpec((tm, d), row), pl.BlockSpec((tm, o.shape[1]), row),
                  pl.BlockSpec((tm, y.shape[1]), row),
                  _const_spec(woo.shape), _const_spec(woy.shape), _const_spec(nw.shape),
                  _const_spec(wg.shape), _const_spec(wu.shape), _const_spec(wd.shape),
                  _const_spec(fw.shape)],
        out_specs=pl.BlockSpec((tm, d), row),
        out_shape=jax.ShapeDtypeStruct((t, d), f32),
        compiler_params=pltpu.CompilerParams(
            dimension_semantics=("arbitrary",), vmem_limit_bytes=VMEM_LIMIT_BYTES),
        name="out_ffn2",
    )(h, o, y, woo, woy, nw, wg, wu, wd, fw)


def kernel(x, positions, ffn1_norm, ffn1_w_gate, ffn1_w_up, ffn1_w_down, mix_norm, w_in, lambda_q1, lambda_k1, lambda_q2, lambda_k2, attn_subln, conv_w, conv_b, dt_bias, a_log, d_skip, ssd_norm, w_out, ffn2_norm, ffn2_w_gate, ffn2_w_up, ffn2_w_down, final_norm):
    batch, seq, d = x.shape
    depth = ffn1_norm.shape[0]
    t = batch * seq
    qk_cols = N_ATTN_HEADS * 2 * ATTN_HEAD_DIM
    attn_width = N_ATTN_HEADS * ATTN_V_DIM
    ssd_width = SSD_HEADS * SSD_HEAD_DIM
    conv_ch = ssd_width + 2 * SSD_GROUPS * SSD_STATE
    splits = [0, qk_cols, 2 * qk_cols, 2 * qk_cols + attn_width, 2 * qk_cols + attn_width + ssd_width,
              2 * qk_cols + attn_width + ssd_width + conv_ch]

    inv = 1.0 / (ROPE_THETA ** (jnp.arange(0, ATTN_HEAD_DIM, 2, dtype=f32) / ATTN_HEAD_DIM))
    invf = jnp.tile(inv, LANES // inv.shape[0])[None, :]
    pos = positions.reshape(t, 1)
    half_dim = ATTN_HEAD_DIM // 2
    qk_order = jnp.arange(qk_cols).reshape(N_ATTN_HEADS, 2, 2, half_dim).transpose(0, 2, 1, 3).reshape(-1)
    pad_heads = lambda a: jnp.pad(a.astype(f32), (0, LANES - SSD_HEADS))[None, :]
    row2d = lambda a: a.astype(f32)[None, :]

    h = x.reshape(t, d)
    for layer in range(depth):
        lambda_init = 0.8 - 0.6 * math.exp(-0.3 * layer)
        last = layer == depth - 1
        h1 = _ffn1(h, row2d(ffn1_norm[layer]), ffn1_w_gate[layer].astype(bf16),
                   ffn1_w_up[layer].astype(bf16), ffn1_w_down[layer].astype(bf16))

        w = w_in[layer]
        wq, wk, wv, wz, wx = (w[:, splits[i]:splits[i + 1]].astype(bf16) for i in range(5))
        wdt = jnp.pad(w[:, splits[5]:], ((0, 0), (0, LANES - SSD_HEADS))).astype(bf16)
        q, k, vt, z, xbc, dt_raw = _in_proj(h1, pos, invf, row2d(mix_norm[layer]),
                                            wq[:, qk_order], wk[:, qk_order], wv.T, wz, wx, wdt)

        lam = _lam(row2d(lambda_q1[layer]), row2d(lambda_k1[layer]), row2d(lambda_q2[layer]),
                   row2d(lambda_k2[layer]), lambda_init)
        o = _attention(lam, q, k, vt, row2d(attn_subln[layer]), batch, seq, 1.0 - lambda_init)

        y = _ssd(xbc, dt_raw, z, conv_w[layer].astype(f32), row2d(conv_b[layer]),
                 pad_heads(dt_bias[layer]), pad_heads(a_log[layer]),
                 row2d(jnp.repeat(d_skip[layer], SSD_HEAD_DIM)), row2d(ssd_norm[layer]), batch, seq)

        wo = w_out[layer].astype(bf16)
        h = _out_ffn2(h1, o, y, wo[:attn_width], wo[attn_width:], row2d(ffn2_norm[layer]),
                      ffn2_w_gate[layer].astype(bf16), ffn2_w_up[layer].astype(bf16),
                      ffn2_w_down[layer].astype(bf16), row2d(final_norm), final=last)
    return h.reshape(batch, seq, d)
```
